```python
import math
import jax, jax.numpy as jnp
from jax import lax
import numpy as np

D_MODEL = 1024
BATCH = 32
SEQ = 2048
DEPTH = 4
DEC_BATCH = 2
DEC_SEQ = 8192
PAST_LEN = 128

HEAD_DIM = 64
DIFF_HEADS = D_MODEL // (4 * HEAD_DIM)
DIFF_VDIM = 2 * HEAD_DIM
DIFF_WIDTH = DIFF_HEADS * DIFF_VDIM
SWA_HEADS = D_MODEL // (2 * HEAD_DIM)
SWA_KV_HEADS = SWA_HEADS // 4
SWA_GROUP = SWA_HEADS // SWA_KV_HEADS
SWA_WIDTH = SWA_HEADS * HEAD_DIM
MIX_WIDTH = DIFF_WIDTH + SWA_WIDTH
WINDOW = 128
BLOCK = 128
QBLOCK = 128
N_EXPERTS = 16
EXPERT_FF = 1024
CAPACITY_FACTOR = 2
ROPE_THETA = 10000.0
EPS = 1e-6
W_DQ = DIFF_HEADS * 2 * HEAD_DIM
W_DK = DIFF_HEADS * 2 * HEAD_DIM
W_DV = DIFF_WIDTH
W_SQ = SWA_HEADS * HEAD_DIM
W_SK = SWA_KV_HEADS * HEAD_DIM
W_SV = SWA_KV_HEADS * HEAD_DIM
IN_WIDTH = W_DQ + W_DK + W_DV + W_SQ + W_SK + W_SV

kernel_name = "hymba_diff_band_gqa_expert_choice_encoder"


def rmsnorm(x, g):
    xf = x.astype(jnp.float32)
    y = xf * lax.rsqrt(jnp.mean(xf * xf, axis=-1, keepdims=True) + EPS) * g.astype(jnp.float32)
    return y.astype(x.dtype)


def rope(x):
    S, d = x.shape[1], x.shape[-1]
    inv = 1.0 / (ROPE_THETA ** (jnp.arange(0, d, 2, dtype=jnp.float32) / d))
    ang = jnp.arange(S, dtype=jnp.float32)[:, None] * inv[None, :]
    ang = jnp.concatenate([ang, ang], axis=-1)
    shape = (1, S) + (1,) * (x.ndim - 3) + (d,)
    cos = jnp.cos(ang).reshape(shape)
    sin = jnp.sin(ang).reshape(shape)
    xf = x.astype(jnp.float32)
    x1, x2 = xf[..., : d // 2], xf[..., d // 2:]
    rot = jnp.concatenate([-x2, x1], axis=-1)
    return (xf * cos + rot * sin).astype(x.dtype)


def diff_attention(q, k, v, lam):
    B, S, H, _, d = q.shape
    nb = S // QBLOCK
    scale = 1.0 / math.sqrt(d)
    qb = q.reshape(B, nb, QBLOCK, H, 2, d).transpose(1, 0, 2, 3, 4, 5)

    def one_block(qblk):
        s = jnp.einsum('bqhcd,bkhcd->bhcqk', qblk, k).astype(jnp.float32) * scale
        p = jax.nn.softmax(s, axis=-1)
        a = p[:, :, 0] - lam * p[:, :, 1]
        return jnp.einsum('bhqk,bkhe->bqhe', a.astype(v.dtype), v)

    o = lax.map(one_block, qb)
    return o.transpose(1, 0, 2, 3, 4).reshape(B, S, H, v.shape[-1])


def band_gqa_attention(q, k, v, sink):
    B, S, Hkv, G, d = q.shape
    nb = S // BLOCK
    scale = 1.0 / math.sqrt(d)
    pad = ((0, 0), (BLOCK, BLOCK), (0, 0), (0, 0))
    kp = jnp.pad(k, pad)
    vp = jnp.pad(v, pad)

    def bands(t):
        return jnp.concatenate(
            [t[:, i * BLOCK: i * BLOCK + S].reshape(B, nb, BLOCK, Hkv, d) for i in range(3)], axis=2)

    kb, vb = bands(kp), bands(vp)
    qb = q.reshape(B, nb, BLOCK, Hkv, G, d)
    s = jnp.einsum('bnqhgd,bnkhd->bnhgqk', qb, kb).astype(jnp.float32) * scale
    blk = jnp.arange(nb)[:, None, None] * BLOCK
    qpos = blk + jnp.arange(BLOCK)[None, :, None]
    kpos = blk - BLOCK + jnp.arange(3 * BLOCK)[None, None, :]
    valid = (jnp.abs(qpos - kpos) <= WINDOW) & (kpos >= 0) & (kpos < S)
    s = jnp.where(valid[None, :, None, None], s, -1e30)
    sink_col = jnp.broadcast_to(sink.astype(jnp.float32)[None, None, :, :, None, None],
                                s.shape[:-1] + (1,))
    p = jax.nn.softmax(jnp.concatenate([s, sink_col], axis=-1), axis=-1)[..., :-1]
    o = jnp.einsum('bnhgqk,bnkhd->bnqhgd', p.astype(v.dtype), vb)
    return o.reshape(B, S, Hkv * G * d)


def expert_choice_ffn(x, w_router, w_gate, w_up, w_down):
    B, S, D = x.shape
    T = B * S
    C = CAPACITY_FACTOR * T // N_EXPERTS
    xt = x.reshape(T, D)
    aff = jax.nn.softmax(jnp.dot(xt, w_router).astype(jnp.float32), axis=-1)
    gates, idx = lax.top_k(aff.T, C)
    xe = xt[idx]
    h = jax.nn.silu(jnp.einsum('ecd,edf->ecf', xe, w_gate)) * jnp.einsum('ecd,edf->ecf', xe, w_up)
    ye = jnp.einsum('ecf,efd->ecd', h, w_down) * gates[..., None].astype(x.dtype)
    y = jnp.zeros((T, D), x.dtype).at[idx.reshape(-1)].add(ye.reshape(-1, D))
    return y.reshape(B, S, D)


def trunk(x, attn_norm, w_in, diff_q_norm, diff_k_norm, lambda_q1, lambda_k1, lambda_q2, lambda_k2,
          diff_subln, swa_q_norm, swa_k_norm, swa_sink, w_out, ffn_norm, w_router, w_gate, w_up, w_down):
    B, S, _ = x.shape
    offs = np.cumsum([W_DQ, W_DK, W_DV, W_SQ, W_SK]).tolist()
    for l in range(DEPTH):
        lam_init = 0.8 - 0.6 * math.exp(-0.3 * l)
        z = jnp.dot(rmsnorm(x, attn_norm[l]), w_in[l])
        dq, dk, dv, sq, sk, sv = jnp.split(z, offs, axis=-1)
        dq = rope(rmsnorm(dq.reshape(B, S, DIFF_HEADS, 2, HEAD_DIM), diff_q_norm[l]))
        dk = rope(rmsnorm(dk.reshape(B, S, DIFF_HEADS, 2, HEAD_DIM), diff_k_norm[l]))
        dv = dv.reshape(B, S, DIFF_HEADS, DIFF_VDIM)
        lam = (jnp.exp(jnp.sum(lambda_q1[l].astype(jnp.float32) * lambda_k1[l].astype(jnp.float32)))
               - jnp.exp(jnp.sum(lambda_q2[l].astype(jnp.float32) * lambda_k2[l].astype(jnp.float32)))
               + lam_init)
        do = diff_attention(dq, dk, dv, lam)
        do = (rmsnorm(do, diff_subln[l]) * (1.0 - lam_init)).reshape(B, S, DIFF_WIDTH)
        sq = rope(rmsnorm(sq.reshape(B, S, SWA_KV_HEADS, SWA_GROUP, HEAD_DIM), swa_q_norm[l]))
        sk = rope(rmsnorm(sk.reshape(B, S, SWA_KV_HEADS, HEAD_DIM), swa_k_norm[l]))
        sv = sv.reshape(B, S, SWA_KV_HEADS, HEAD_DIM)
        so = band_gqa_attention(sq, sk, sv, swa_sink[l].reshape(SWA_KV_HEADS, SWA_GROUP))
        x = x + jnp.dot(jnp.concatenate([do, so], axis=-1), w_out[l])
        x = x + expert_choice_ffn(rmsnorm(x, ffn_norm[l]), w_router[l], w_gate[l], w_up[l], w_down[l])
    return x


def setup_inputs(seed: int = 0) -> dict:
    key = jax.random.key(seed)
    ks = jax.random.split(key, 24)
    f32 = jnp.float32
    nrm = lambda k, shape, s: jax.random.normal(k, shape, f32) * s
    gain = lambda k, shape: 1.0 + 0.05 * jax.random.normal(k, shape, f32)
    return {
        "x_prompt": nrm(ks[0], (BATCH, SEQ, D_MODEL), 1.0),
        "x_sample": nrm(ks[1], (DEC_BATCH, DEC_SEQ, D_MODEL), 1.0),
        "attn_norm": gain(ks[2], (DEPTH, D_MODEL)),
        "w_in": nrm(ks[3], (DEPTH, D_MODEL, IN_WIDTH), D_MODEL ** -0.5),
        "diff_q_norm": gain(ks[4], (DEPTH, HEAD_DIM)),
        "diff_k_norm": gain(ks[5], (DEPTH, HEAD_DIM)),
        "lambda_q1": nrm(ks[6], (DEPTH, HEAD_DIM), 0.1),
        "lambda_k1": nrm(ks[7], (DEPTH, HEAD_DIM), 0.1),
        "lambda_q2": nrm(ks[8], (DEPTH, HEAD_DIM), 0.1),
        "lambda_k2": nrm(ks[9], (DEPTH, HEAD_DIM), 0.1),
        "diff_subln": gain(ks[10], (DEPTH, DIFF_VDIM)),
        "swa_q_norm": gain(ks[11], (DEPTH, HEAD_DIM)),
        "swa_k_norm": gain(ks[12], (DEPTH, HEAD_DIM)),
        "swa_sink": nrm(ks[13], (DEPTH, SWA_HEADS), 0.5),
        "w_out": nrm(ks[14], (DEPTH, MIX_WIDTH, D_MODEL), MIX_WIDTH ** -0.5),
        "ffn_norm": gain(ks[15], (DEPTH, D_MODEL)),
        "w_router": nrm(ks[16], (DEPTH, D_MODEL, N_EXPERTS), D_MODEL ** -0.5),
        "w_gate": nrm(ks[17], (DEPTH, N_EXPERTS, D_MODEL, EXPERT_FF), D_MODEL ** -0.5),
        "w_up": nrm(ks[18], (DEPTH, N_EXPERTS, D_MODEL, EXPERT_FF), D_MODEL ** -0.5),
        "w_down": nrm(ks[19], (DEPTH, N_EXPERTS, EXPERT_FF, D_MODEL), EXPERT_FF ** -0.5),
    }


def reference(x_prompt, x_sample, attn_norm, w_in, diff_q_norm, diff_k_norm, lambda_q1, lambda_k1,
              lambda_q2, lambda_k2, diff_subln, swa_q_norm, swa_k_norm, swa_sink, w_out, ffn_norm,
              w_router, w_gate, w_up, w_down):
    y_prompt = trunk(x_prompt, attn_norm, w_in, diff_q_norm, diff_k_norm, lambda_q1, lambda_k1,
                     lambda_q2, lambda_k2, diff_subln, swa_q_norm, swa_k_norm, swa_sink, w_out,
                     ffn_norm, w_router, w_gate, w_up, w_down)
    y_sample = trunk(x_sample, attn_norm, w_in, diff_q_norm, diff_k_norm, lambda_q1, lambda_k1,
                     lambda_q2, lambda_k2, diff_subln, swa_q_norm, swa_k_norm, swa_sink, w_out,
                     ffn_norm, w_router, w_gate, w_up, w_down)
    return (y_prompt, y_sample)
```

```python
import functools
import math

import jax
import jax.numpy as jnp
from jax import lax
from jax.experimental import pallas as pl
from jax.experimental.pallas import tpu as pltpu

D_MODEL = 1024
HEAD_DIM = 64
DIFF_HEADS = 4
DIFF_VDIM = 128
DIFF_WIDTH = 512
SWA_HEADS = 8
SWA_KV_HEADS = 2
SWA_GROUP = 4
SWA_WIDTH = 512
BLOCK = 128
N_EXPERTS = 16
EXPERT_FF = 1024
CAPACITY_FACTOR = 2
ROPE_THETA = 10000.0
EPS = 1e-6
IN_WIDTH = 2304
LANES = 128
AUX = LANES
ROW_W = D_MODEL + AUX
RANK_CHUNK = 256
VMEM_LIMIT = 56 * 1024 * 1024

f32 = jnp.float32
bf16 = jnp.bfloat16


def _cparams(sem):
    return pltpu.CompilerParams(dimension_semantics=sem, vmem_limit_bytes=VMEM_LIMIT)


def _lane_iota(shape):
    return lax.broadcasted_iota(jnp.int32, shape, len(shape) - 1)


def _head_norm_rope(u, gain, gmat, cos, sin_signed):
    w = u.shape[1]
    ms = jnp.dot((u * u).astype(bf16), gmat, preferred_element_type=f32)
    un = u * lax.rsqrt(ms + EPS) * gain
    lane = _lane_iota(un.shape)
    first_half = (lane % HEAD_DIM) < (HEAD_DIM // 2)
    rot = jnp.where(first_half, pltpu.roll(un, w - HEAD_DIM // 2, 1), pltpu.roll(un, HEAD_DIM // 2, 1))
    return un * cos + rot * sin_signed


def _attn_in_kernel(x_ref, g_ref, w_ref, cos_ref, sin_ref, gdq_ref, gdk_ref, gsq_ref, gsk_ref, gm512_ref, gm128_ref,
                    dq_ref, dk_ref, dv_ref, sq_ref, sk_ref, sv_ref):
    x = x_ref[...]
    ms = jnp.mean(x * x, axis=-1, keepdims=True)
    xn = (x * lax.rsqrt(ms + EPS) * g_ref[...]).astype(bf16)
    z = jnp.dot(xn, w_ref[...], preferred_element_type=f32)
    cos1, sin1 = cos_ref[...], sin_ref[...]
    cos4 = jnp.concatenate([cos1] * 4, axis=1)
    sin4 = jnp.concatenate([sin1] * 4, axis=1)
    gm512 = gm512_ref[...]
    scale = 1.0 / math.sqrt(HEAD_DIM)
    dq = _head_norm_rope(z[:, 0:512], gdq_ref[...], gm512, cos4, sin4)
    dk = _head_norm_rope(z[:, 512:1024], gdk_ref[...], gm512, cos4, sin4)
    sq = _head_norm_rope(z[:, 1536:2048], gsq_ref[...], gm512, cos4, sin4)
    sk = _head_norm_rope(z[:, 2048:2176], gsk_ref[...], gm128_ref[...], cos1, sin1)
    dq_ref[...] = (dq * scale).astype(bf16)
    dk_ref[...] = dk.astype(bf16)
    dv_ref[...] = z[:, 1024:1536].astype(bf16)
    sk_ref[...] = sk.astype(bf16)
    sv_ref[...] = z[:, 2176:2304].astype(bf16)
    sq = sq * scale
    lane = _lane_iota((sq.shape[0], LANES))
    for j in range(SWA_HEADS):
        blk = sq[:, (j // 2) * LANES:(j // 2 + 1) * LANES]
        hk = j // SWA_GROUP
        if (j % 2) != hk:
            blk = pltpu.roll(blk, HEAD_DIM, 1)
        blk = jnp.where((lane // HEAD_DIM) == hk, blk, 0.0)
        sq_ref[:, j * LANES:(j + 1) * LANES] = blk.astype(bf16)


def _attn_in(x2, g, w_in, cos, sin, gdq, gdk, gsq, gsk, gm512, gm128, seq, tb):
    t = x2.shape[0]
    nb_seq = seq // tb
    row = lambda i: (i, 0)
    const = lambda i: (0, 0)
    outs = [jax.ShapeDtypeStruct((t, 512), bf16)] * 3 + [jax.ShapeDtypeStruct((t, 1024), bf16)] + \
           [jax.ShapeDtypeStruct((t, 128), bf16)] * 2
    return pl.pallas_call(
        _attn_in_kernel,
        grid=(t // tb,),
        in_specs=[
            pl.BlockSpec((tb, D_MODEL), row),
            pl.BlockSpec((1, D_MODEL), const),
            pl.BlockSpec((D_MODEL, IN_WIDTH), const),
            pl.BlockSpec((tb, LANES), lambda i: (i % nb_seq, 0)),
            pl.BlockSpec((tb, LANES), lambda i: (i % nb_seq, 0)),
            pl.BlockSpec((1, 512), const), pl.BlockSpec((1, 512), const), pl.BlockSpec((1, 512), const),
            pl.BlockSpec((1, 128), const),
            pl.BlockSpec((512, 512), const), pl.BlockSpec((128, 128), const),
        ],
        out_specs=[pl.BlockSpec((tb, 512), row)] * 3 + [pl.BlockSpec((tb, 1024), row)] + [pl.BlockSpec((tb, 128), row)] * 2,
        out_shape=outs,
        compiler_params=_cparams(("arbitrary",)),
        name="attn_in",
    )(x2, g, w_in, cos, sin, gdq, gdk, gsq, gsk, gm512, gm128)


def _diff_attn_kernel(scal_ref, q_ref, k_ref, v_ref, g_ref, o_ref):
    lam = scal_ref[0]
    out_scale = scal_ref[1]
    q = q_ref[...]
    tq = q.shape[0]
    lane = _lane_iota(q.shape)
    zero = jnp.zeros_like(q)
    qs = jnp.concatenate([jnp.where(lane < HEAD_DIM, q, zero), jnp.where(lane >= HEAD_DIM, q, zero)], axis=0)
    s = lax.dot_general(qs, k_ref[...], (((1,), (1,)), ((), ())), preferred_element_type=f32)
    m = jnp.max(s, axis=-1, keepdims=True)
    e = jnp.exp(s - m)
    l = jnp.sum(e, axis=-1, keepdims=True)
    p = e * (1.0 / l)
    a = (p[:tq] - lam * p[tq:]).astype(bf16)
    o = jnp.dot(a, v_ref[...], preferred_element_type=f32)
    ms = jnp.mean(o * o, axis=-1, keepdims=True)
    o_ref[...] = ((o * lax.rsqrt(ms + EPS) * g_ref[...]) * out_scale).astype(bf16)


def _diff_attn(scal, dq, dk, dv, g, batch, seq, tq):
    t = dq.shape[0]
    nq = seq // tq
    return pl.pallas_call(
        _diff_attn_kernel,
        grid=(batch, DIFF_HEADS, nq),
        in_specs=[
            pl.BlockSpec(memory_space=pltpu.SMEM),
            pl.BlockSpec((tq, LANES), lambda b, h, i: (b * nq + i, h)),
            pl.BlockSpec((seq, LANES), lambda b, h, i: (b, h)),
            pl.BlockSpec((seq, LANES), lambda b, h, i: (b, h)),
            pl.BlockSpec((1, LANES), lambda b, h, i: (0, 0)),
        ],
        out_specs=pl.BlockSpec((tq, LANES), lambda b, h, i: (b * nq + i, h)),
        out_shape=jax.ShapeDtypeStruct((t, DIFF_WIDTH), bf16),
        compiler_params=_cparams(("arbitrary", "arbitrary", "arbitrary")),
        name="diff_attn",
    )(scal, dq, dk, dv, g)


def _band_attn_kernel(sink_ref, q_ref, kp_ref, kc_ref, kn_ref, vp_ref, vc_ref, vn_ref, o_ref):
    n = pl.program_id(1)
    nb = pl.num_programs(1)
    k3 = jnp.concatenate([kp_ref[...], kc_ref[...], kn_ref[...]], axis=0)
    v3 = jnp.concatenate([vp_ref[...], vc_ref[...], vn_ref[...]], axis=0)
    r = lax.broadcasted_iota(jnp.int32, (BLOCK, 3 * BLOCK), 0)
    c = lax.broadcasted_iota(jnp.int32, (BLOCK, 3 * BLOCK), 1)
    valid = (c >= r) & (c <= r + 2 * BLOCK)
    valid = valid & ((c >= BLOCK) | (n > 0)) & ((c < 2 * BLOCK) | (n < nb - 1))
    lane = _lane_iota((BLOCK, LANES))
    outs = []
    for j in range(SWA_HEADS):
        qj = q_ref[:, j * LANES:(j + 1) * LANES]
        s = lax.dot_general(qj, k3, (((1,), (1,)), ((), ())), preferred_element_type=f32)
        s = jnp.where(valid, s, -1e30)
        sink = sink_ref[j]
        m = jnp.maximum(jnp.max(s, axis=-1, keepdims=True), sink)
        e = jnp.exp(s - m)
        l = jnp.sum(e, axis=-1, keepdims=True) + jnp.exp(sink - m)
        p = (e * (1.0 / l)).astype(bf16)
        outs.append(jnp.dot(p, v3, preferred_element_type=f32))
    for pc in range(SWA_HEADS // 2):
        hk = (2 * pc) // SWA_GROUP
        left, right = outs[2 * pc], outs[2 * pc + 1]
        if hk == 1:
            left = pltpu.roll(left, HEAD_DIM, 1)
        else:
            right = pltpu.roll(right, HEAD_DIM, 1)
        o_ref[:, pc * LANES:(pc + 1) * LANES] = jnp.where(lane < HEAD_DIM, left, right).astype(bf16)


def _band_attn(sink, sqp, sk, sv, batch, seq):
    t = sqp.shape[0]
    nb = seq // BLOCK
    prev = lambda b, n: (b * nb + jnp.maximum(n - 1, 0), 0)
    cur = lambda b, n: (b * nb + n, 0)
    nxt = lambda b, n: (b * nb + jnp.minimum(n + 1, nb - 1), 0)
    kv = lambda f: pl.BlockSpec((BLOCK, LANES), f)
    return pl.pallas_call(
        _band_attn_kernel,
        grid=(batch, nb),
        in_specs=[pl.BlockSpec(memory_space=pltpu.SMEM), pl.BlockSpec((BLOCK, SWA_HEADS * LANES), cur),
                  kv(prev), kv(cur), kv(nxt), kv(prev), kv(cur), kv(nxt)],
        out_specs=pl.BlockSpec((BLOCK, SWA_WIDTH), cur),
        out_shape=jax.ShapeDtypeStruct((t, SWA_WIDTH), bf16),
        compiler_params=_cparams(("arbitrary", "arbitrary")),
        name="band_attn",
    )(sink, sqp, sk, sk, sk, sv, sv, sv)


def _attn_out_kernel(do_ref, so_ref, x_ref, wa_ref, wb_ref, g_ref, rhi_ref, rlo_ref, h_ref, hn_ref, aff_ref):
    h = x_ref[...] + jnp.dot(do_ref[...], wa_ref[...], preferred_element_type=f32) \
        + jnp.dot(so_ref[...], wb_ref[...], preferred_element_type=f32)
    h_ref[...] = h
    ms = jnp.mean(h * h, axis=-1, keepdims=True)
    hn = h * lax.rsqrt(ms + EPS) * g_ref[...]
    hn_ref[:, 0:D_MODEL] = hn
    xh = hn.astype(bf16)
    xl = (hn - xh.astype(f32)).astype(bf16)
    rhi, rlo = rhi_ref[...], rlo_ref[...]
    logits = jnp.dot(xh, rhi, preferred_element_type=f32) + (jnp.dot(xh, rlo, preferred_element_type=f32)
                                                             + jnp.dot(xl, rhi, preferred_element_type=f32))
    lane = _lane_iota(logits.shape)
    live = lane < N_EXPERTS
    lm = jnp.where(live, logits, -1e30)
    m = jnp.max(lm, axis=-1, keepdims=True)
    e = jnp.where(live, jnp.exp(lm - m), 0.0)
    aff = e / jnp.sum(e, axis=-1, keepdims=True)
    hn_ref[:, D_MODEL:ROW_W] = aff
    aff_ref[...] = aff[:, 0:N_EXPERTS]


def _attn_out(do, so, x2, wa, wb, g, rhi, rlo, tb):
    t = x2.shape[0]
    row = lambda i: (i, 0)
    const = lambda i: (0, 0)
    return pl.pallas_call(
        _attn_out_kernel,
        grid=(t // tb,),
        in_specs=[pl.BlockSpec((tb, 512), row), pl.BlockSpec((tb, 512), row), pl.BlockSpec((tb, D_MODEL), row),
                  pl.BlockSpec((512, D_MODEL), const), pl.BlockSpec((512, D_MODEL), const),
                  pl.BlockSpec((1, D_MODEL), const), pl.BlockSpec((D_MODEL, LANES), const),
                  pl.BlockSpec((D_MODEL, LANES), const)],
        out_specs=[pl.BlockSpec((tb, D_MODEL), row), pl.BlockSpec((tb, ROW_W), row), pl.BlockSpec((tb, N_EXPERTS), row)],
        out_shape=[jax.ShapeDtypeStruct((t, D_MODEL), f32), jax.ShapeDtypeStruct((t, ROW_W), f32),
                   jax.ShapeDtypeStruct((t, N_EXPERTS), f32)],
        compiler_params=_cparams(("arbitrary",)),
        name="attn_out",
    )(do, so, x2, wa, wb, g, rhi, rlo)


def _class_sum(v):
    v = v + pltpu.roll(v, 64, 1)
    v = v + pltpu.roll(v, 32, 1)
    return v + pltpu.roll(v, 16, 1)


def _select_kernel(a_ref, o_ref, *, capacity):
    bits = pltpu.bitcast(a_ref[...], jnp.int32)

    def count_ge(cand):
        cnt = jnp.sum((bits >= cand[0:1, :]).astype(f32), axis=0, keepdims=True)
        return _class_sum(jnp.broadcast_to(cnt, (8, LANES)))

    def body(i, thr):
        cand = thr | jnp.left_shift(jnp.int32(1), 30 - i)
        return jnp.where(count_ge(cand) >= capacity, cand, thr)

    thr = lax.fori_loop(0, 31, body, jnp.zeros((8, LANES), jnp.int32))
    n_gt = count_ge(thr + 1)
    row = lax.broadcasted_iota(jnp.int32, (8, LANES), 0)
    o_ref[...] = jnp.where(row == 0, pltpu.bitcast(thr, f32), capacity - n_gt)


def _select(aff_dense, capacity):
    return pl.pallas_call(
        functools.partial(_select_kernel, capacity=float(capacity)),
        out_shape=jax.ShapeDtypeStruct((8, LANES), f32),
        compiler_params=pltpu.CompilerParams(vmem_limit_bytes=VMEM_LIMIT),
        name="select",
    )(aff_dense)


def _rank_kernel(aff_ref, thr_ref, need_ref, tok_ref, carry_ref, offs_ref):
    c = pl.program_id(0)

    @pl.when(c == 0)
    def _():
        tok_ref[...] = jnp.zeros_like(tok_ref)
        carry_ref[...] = jnp.zeros_like(carry_ref)
        for e in range(N_EXPERTS):
            offs_ref[e] = 0

    a = aff_ref[...]
    thr = thr_ref[...]
    gt = a > thr
    eq = a == thr
    ri = lax.broadcasted_iota(jnp.int32, (RANK_CHUNK, RANK_CHUNK), 0)
    ci = lax.broadcasted_iota(jnp.int32, (RANK_CHUNK, RANK_CHUNK), 1)
    lower = (ci < ri).astype(bf16)
    eqf = eq.astype(bf16)
    eq_rank = jnp.dot(lower, eqf, preferred_element_type=f32) + carry_ref[0:1, :]
    sel = gt | (eq & (eq_rank < need_ref[...]))
    self_ = sel.astype(bf16)
    rank = jnp.dot(lower, self_, preferred_element_type=f32) + carry_ref[1:2, :]
    n_sel = jnp.sum(sel.astype(f32), axis=0, keepdims=True)
    carry_ref[0:1, :] = carry_ref[0:1, :] + jnp.sum(eq.astype(f32), axis=0, keepdims=True)
    carry_ref[1:2, :] = carry_ref[1:2, :] + n_sel

    sub = lax.broadcasted_iota(jnp.int32, (8, RANK_CHUNK), 0)
    loc = lax.broadcasted_iota(jnp.int32, (8, RANK_CHUNK), 1)
    lhs = jnp.where(sub == 0, loc, jnp.where(sub == 1, 1, 0)).astype(f32).astype(bf16)
    slot = lax.broadcasted_iota(jnp.int32, (RANK_CHUNK, 3 * LANES), 1).astype(f32)
    base = (c * RANK_CHUNK).astype(f32)
    for e in range(N_EXPERTS):
        off = offs_ref[e]
        j0 = off // LANES
        rel = rank[:, e:e + 1] - (j0 * LANES).astype(f32)
        onehot = jnp.where((rel == slot) & sel[:, e:e + 1], 1.0, 0.0).astype(bf16)
        placed = jnp.dot(lhs, onehot, preferred_element_type=f32)
        row = lax.broadcasted_iota(jnp.int32, placed.shape, 0)
        filled = jnp.broadcast_to(placed[1:2, :], placed.shape)
        placed = jnp.where(row == 0, placed + base * filled, placed)
        for k in range(3):
            tok_ref[e, j0 + k] = tok_ref[e, j0 + k] + placed[:, k * LANES:(k + 1) * LANES]
        offs_ref[e] = off + jnp.sum(n_sel[:, e:e + 1]).astype(jnp.int32)


def _rank(aff16, thr, need, capacity):
    t = aff16.shape[0]
    n_tiles = capacity // LANES + 3
    const2 = lambda c: (0, 0)
    return pl.pallas_call(
        _rank_kernel,
        grid=(t // RANK_CHUNK,),
        in_specs=[pl.BlockSpec((RANK_CHUNK, N_EXPERTS), lambda c: (c, 0)),
                  pl.BlockSpec((1, N_EXPERTS), const2), pl.BlockSpec((1, N_EXPERTS), const2)],
        out_specs=pl.BlockSpec((N_EXPERTS, n_tiles, 8, LANES), lambda c: (0, 0, 0, 0)),
        out_shape=jax.ShapeDtypeStruct((N_EXPERTS, n_tiles, 8, LANES), f32),
        scratch_shapes=[pltpu.VMEM((8, N_EXPERTS), f32), pltpu.SMEM((N_EXPERTS,), jnp.int32)],
        compiler_params=_cparams(("arbitrary",)),
        name="rank",
    )(aff16, thr, need)


def _ffn_kernel(tok_ref, rows_ref, y_in_ref, wg_ref, wu_ref, wd_ref, y_ref, xbuf, abuf, sem):
    del y_in_ref
    e = pl.program_id(0)
    tc = xbuf.shape[0]

    def x_copy(r):
        return pltpu.make_async_copy(rows_ref.at[pl.ds(tok_ref[0, 0, r], 1)], xbuf.at[pl.ds(r, 1)], sem.at[0])

    def acc_copy(r):
        return pltpu.make_async_copy(y_ref.at[pl.ds(tok_ref[0, 0, r], 1)], abuf.at[pl.ds(r, 1)], sem.at[1])

    def out_copy(r):
        return pltpu.make_async_copy(abuf.at[pl.ds(r, 1)], y_ref.at[pl.ds(tok_ref[0, 0, r], 1)], sem.at[2])

    def start_in(r, carry):
        x_copy(r).start()
        acc_copy(r).start()
        return carry

    def wait_in(r, carry):
        x_copy(r).wait()
        acc_copy(r).wait()
        return carry

    lax.fori_loop(0, tc, start_in, 0)
    lax.fori_loop(0, tc, wait_in, 0)

    x = xbuf[:, 0:D_MODEL].astype(bf16)
    aux = xbuf[:, D_MODEL:ROW_W]
    gate = jnp.sum(jnp.where(_lane_iota(aux.shape) == e, aux, 0.0), axis=-1, keepdims=True)
    g = jnp.dot(x, wg_ref[0], preferred_element_type=f32)
    u = jnp.dot(x, wu_ref[0], preferred_element_type=f32)
    hmid = (g * jax.nn.sigmoid(g) * u).astype(bf16)
    y = jnp.dot(hmid, wd_ref[0], preferred_element_type=f32)
    abuf[...] = abuf[...] + y * gate

    def start_out(r, carry):
        out_copy(r).start()
        return carry

    def wait_out(r, carry):
        out_copy(r).wait()
        return carry

    lax.fori_loop(0, tc, start_out, 0)
    lax.fori_loop(0, tc, wait_out, 0)


def _ffn(tok, rows, h, wg, wu, wd, capacity, tc):
    t = h.shape[0]
    nt = capacity // tc
    wspec = pl.BlockSpec((1, D_MODEL, EXPERT_FF), lambda e, j: (e, 0, 0))
    return pl.pallas_call(
        _ffn_kernel,
        grid=(N_EXPERTS, nt),
        in_specs=[pl.BlockSpec((1, 1, tc), lambda e, j: (e * nt + j, 0, 0), memory_space=pltpu.SMEM),
                  pl.BlockSpec(memory_space=pl.ANY), pl.BlockSpec(memory_space=pl.ANY),
                  wspec, wspec, pl.BlockSpec((1, EXPERT_FF, D_MODEL), lambda e, j: (e, 0, 0))],
        out_specs=pl.BlockSpec(memory_space=pl.ANY),
        out_shape=jax.ShapeDtypeStruct((t, D_MODEL), f32),
        scratch_shapes=[pltpu.VMEM((tc, ROW_W), f32), pltpu.VMEM((tc, D_MODEL), f32), pltpu.SemaphoreType.DMA((3,))],
        input_output_aliases={2: 0},
        compiler_params=_cparams(("arbitrary", "arbitrary")),
        name="ffn",
    )(tok, rows, h, wg, wu, wd)


def _rope_tables(seq):
    half = HEAD_DIM // 2
    inv = 1.0 / (ROPE_THETA ** (jnp.arange(0, HEAD_DIM, 2, dtype=f32) / HEAD_DIM))
    ang = jnp.arange(seq, dtype=f32)[:, None] * inv[None, :]
    cos = jnp.tile(jnp.cos(ang), (1, LANES // half))
    sign = jnp.where((jnp.arange(LANES) % HEAD_DIM) < half, -1.0, 1.0).astype(f32)
    sin = jnp.tile(jnp.sin(ang), (1, LANES // half)) * sign[None, :]
    return cos, sin


def _block_diag_mean(width):
    i = jnp.arange(width)
    return jnp.where((i[:, None] // HEAD_DIM) == (i[None, :] // HEAD_DIM), 1.0 / HEAD_DIM, 0.0).astype(bf16)


def _pick_tile(n, pref):
    while n % pref:
        pref //= 2
    return pref


def _trunk(x, p):
    batch, seq, _ = x.shape
    t = batch * seq
    capacity = CAPACITY_FACTOR * t // N_EXPERTS
    assert seq % BLOCK == 0 and t % RANK_CHUNK == 0 and capacity % LANES == 0
    tb = _pick_tile(seq, 512)
    tq = _pick_tile(seq, 256 if seq <= 2048 else 128)
    tc = _pick_tile(capacity, 256)
    cos, sin = _rope_tables(seq)
    gm512, gm128 = _block_diag_mean(512), _block_diag_mean(128)
    x2 = x.reshape(t, D_MODEL)
    depth = p["w_in"].shape[0]
    for l in range(depth):
        lam_init = 0.8 - 0.6 * math.exp(-0.3 * l)
        lam = (jnp.exp(jnp.sum(p["lambda_q1"][l] * p["lambda_k1"][l]))
               - jnp.exp(jnp.sum(p["lambda_q2"][l] * p["lambda_k2"][l])) + lam_init)
        scal = jnp.stack([lam, jnp.asarray(1.0 - lam_init, f32)]).astype(f32)
        tile8 = lambda v: jnp.tile(v, 8)[None, :]
        dq, dk, dv, sqp, sk, sv = _attn_in(
            x2, p["attn_norm"][l][None, :], p["w_in_bf"][l], cos, sin,
            tile8(p["diff_q_norm"][l]), tile8(p["diff_k_norm"][l]), tile8(p["swa_q_norm"][l]),
            jnp.tile(p["swa_k_norm"][l], 2)[None, :], gm512, gm128, seq, tb)
        do = _diff_attn(scal, dq, dk, dv, p["diff_subln"][l][None, :], batch, seq, tq)
        so = _band_attn(p["swa_sink"][l], sqp, sk, sv, batch, seq)
        h, rows, aff16 = _attn_out(do, so, x2, p["w_out_bf"][l, :DIFF_WIDTH], p["w_out_bf"][l, DIFF_WIDTH:],
                                   p["ffn_norm"][l][None, :], p["r_hi"][l], p["r_lo"][l], tb)
        sel = _select(aff16.reshape(t // 8, LANES), capacity)
        tokbuf = _rank(aff16, sel[0:1, :N_EXPERTS], sel[1:2, :N_EXPERTS], capacity)
        tok = tokbuf[:, :capacity // LANES, 0, :].reshape(N_EXPERTS * capacity // tc, 1, tc).astype(jnp.int32)
        x2 = _ffn(tok, rows, h, p["w_gate_bf"][l], p["w_up_bf"][l], p["w_down_bf"][l], capacity, tc)
    return x2.reshape(batch, seq, D_MODEL)


def kernel(x_prompt, x_sample, attn_norm, w_in, diff_q_norm, diff_k_norm, lambda_q1, lambda_k1, lambda_q2, lambda_k2,
           diff_subln, swa_q_norm, swa_k_norm, swa_sink, w_out, ffn_norm, w_router, w_gate, w_up, w_down):
    r_pad = jnp.pad(w_router, ((0, 0), (0, 0), (0, LANES - N_EXPERTS)))
    r_hi = r_pad.astype(bf16)
    p = dict(attn_norm=attn_norm, diff_q_norm=diff_q_norm, diff_k_norm=diff_k_norm, lambda_q1=lambda_q1,
             lambda_k1=lambda_k1, lambda_q2=lambda_q2, lambda_k2=lambda_k2, diff_subln=diff_subln,
             swa_q_norm=swa_q_norm, swa_k_norm=swa_k_norm, swa_sink=swa_sink, ffn_norm=ffn_norm,
             w_in=w_in, w_in_bf=w_in.astype(bf16), w_out_bf=w_out.astype(bf16),
             r_hi=r_hi, r_lo=(r_pad - r_hi.astype(f32)).astype(bf16),
             w_gate_bf=w_gate.astype(bf16), w_up_bf=w_up.astype(bf16), w_down_bf=w_down.astype(bf16))
    return (_trunk(x_prompt, p), _trunk(x_sample, p))
```

```python
import functools
import math

import jax
import jax.numpy as jnp
from jax import lax
from jax.experimental import pallas as pl
from jax.experimental.pallas import tpu as pltpu

D_MODEL = 1024
HEAD_DIM = 64
DIFF_HEADS = 4
DIFF_VDIM = 128
DIFF_WIDTH = 512
SWA_HEADS = 8
SWA_KV_HEADS = 2
SWA_GROUP = 4
SWA_WIDTH = 512
BLOCK = 128
N_EXPERTS = 16
EXPERT_FF = 1024
CAPACITY_FACTOR = 2
ROPE_THETA = 10000.0
EPS = 1e-6
IN_WIDTH = 2304
LANES = 128
ROW_TILES = D_MODEL // LANES
RANK_CHUNK = 256
FFN_TILE = 512
VMEM_LIMIT = 56 * 1024 * 1024

f32 = jnp.float32
bf16 = jnp.bfloat16


def _cparams(sem):
    return pltpu.CompilerParams(dimension_semantics=sem, vmem_limit_bytes=VMEM_LIMIT)


def _lane_iota(shape):
    return lax.broadcasted_iota(jnp.int32, shape, len(shape) - 1)


def _head_norm_rope(u, gain, gmat, cos, sin_signed):
    w = u.shape[1]
    ms = jnp.dot((u * u).astype(bf16), gmat, preferred_element_type=f32)
    un = u * lax.rsqrt(ms + EPS) * gain
    lane = _lane_iota(un.shape)
    first_half = (lane % HEAD_DIM) < (HEAD_DIM // 2)
    rot = jnp.where(first_half, pltpu.roll(un, w - HEAD_DIM // 2, 1), pltpu.roll(un, HEAD_DIM // 2, 1))
    return un * cos + rot * sin_signed


def _attn_in_kernel(x_ref, g_ref, w_ref, cos_ref, sin_ref, gdq_ref, gdk_ref, gsq_ref, gsk_ref, gm512_ref, gm128_ref,
                    dq_ref, dk_ref, dv_ref, sq_ref, sk_ref, sv_ref):
    x = x_ref[...]
    ms = jnp.mean(x * x, axis=-1, keepdims=True)
    xn = (x * lax.rsqrt(ms + EPS) * g_ref[...]).astype(bf16)
    z = jnp.dot(xn, w_ref[...], preferred_element_type=f32)
    cos1, sin1 = cos_ref[...], sin_ref[...]
    cos4 = jnp.concatenate([cos1] * 4, axis=1)
    sin4 = jnp.concatenate([sin1] * 4, axis=1)
    gm512 = gm512_ref[...]
    scale = 1.0 / math.sqrt(HEAD_DIM)
    dq = _head_norm_rope(z[:, 0:512], gdq_ref[...], gm512, cos4, sin4)
    dk = _head_norm_rope(z[:, 512:1024], gdk_ref[...], gm512, cos4, sin4)
    sq = _head_norm_rope(z[:, 1536:2048], gsq_ref[...], gm512, cos4, sin4)
    sk = _head_norm_rope(z[:, 2048:2176], gsk_ref[...], gm128_ref[...], cos1, sin1)
    dq_ref[...] = (dq * scale).astype(bf16)
    dk_ref[...] = dk.astype(bf16)
    dv_ref[...] = z[:, 1024:1536].astype(bf16)
    sk_ref[...] = sk.astype(bf16)
    sv_ref[...] = z[:, 2176:2304].astype(bf16)
    sq = sq * scale
    lane = _lane_iota((sq.shape[0], LANES))
    for j in range(SWA_HEADS):
        blk = sq[:, (j // 2) * LANES:(j // 2 + 1) * LANES]
        hk = j // SWA_GROUP
        if (j % 2) != hk:
            blk = pltpu.roll(blk, HEAD_DIM, 1)
        blk = jnp.where((lane // HEAD_DIM) == hk, blk, 0.0)
        sq_ref[:, j * LANES:(j + 1) * LANES] = blk.astype(bf16)


def _attn_in(x2, g, w_in, cos, sin, gdq, gdk, gsq, gsk, gm512, gm128, seq, tb):
    t = x2.shape[0]
    nb_seq = seq // tb
    row = lambda i: (i, 0)
    const = lambda i: (0, 0)
    outs = [jax.ShapeDtypeStruct((t, 512), bf16)] * 3 + [jax.ShapeDtypeStruct((t, 1024), bf16)] + \
           [jax.ShapeDtypeStruct((t, 128), bf16)] * 2
    return pl.pallas_call(
        _attn_in_kernel,
        grid=(t // tb,),
        in_specs=[
            pl.BlockSpec((tb, D_MODEL), row),
            pl.BlockSpec((1, D_MODEL), const),
            pl.BlockSpec((D_MODEL, IN_WIDTH), const),
            pl.BlockSpec((tb, LANES), lambda i: (i % nb_seq, 0)),
            pl.BlockSpec((tb, LANES), lambda i: (i % nb_seq, 0)),
            pl.BlockSpec((1, 512), const), pl.BlockSpec((1, 512), const), pl.BlockSpec((1, 512), const),
            pl.BlockSpec((1, 128), const),
            pl.BlockSpec((512, 512), const), pl.BlockSpec((128, 128), const),
        ],
        out_specs=[pl.BlockSpec((tb, 512), row)] * 3 + [pl.BlockSpec((tb, 1024), row)] + [pl.BlockSpec((tb, 128), row)] * 2,
        out_shape=outs,
        compiler_params=_cparams(("arbitrary",)),
        name="attn_in",
    )(x2, g, w_in, cos, sin, gdq, gdk, gsq, gsk, gm512, gm128)


def _diff_attn_kernel(scal_ref, q_ref, k_ref, v_ref, g_ref, o_ref):
    lam = scal_ref[0]
    out_scale = scal_ref[1]
    q = q_ref[...]
    tq = q.shape[0]
    lane = _lane_iota(q.shape)
    zero = jnp.zeros_like(q)
    qs = jnp.concatenate([jnp.where(lane < HEAD_DIM, q, zero), jnp.where(lane >= HEAD_DIM, q, zero)], axis=0)
    s = lax.dot_general(qs, k_ref[...], (((1,), (1,)), ((), ())), preferred_element_type=f32)
    m = jnp.max(s, axis=-1, keepdims=True)
    e = jnp.exp(s - m)
    l = jnp.sum(e, axis=-1, keepdims=True)
    p = e * (1.0 / l)
    a = (p[:tq] - lam * p[tq:]).astype(bf16)
    o = jnp.dot(a, v_ref[...], preferred_element_type=f32)
    ms = jnp.mean(o * o, axis=-1, keepdims=True)
    o_ref[...] = ((o * lax.rsqrt(ms + EPS) * g_ref[...]) * out_scale).astype(bf16)


def _diff_attn(scal, dq, dk, dv, g, batch, seq, tq):
    t = dq.shape[0]
    nq = seq // tq
    return pl.pallas_call(
        _diff_attn_kernel,
        grid=(batch, DIFF_HEADS, nq),
        in_specs=[
            pl.BlockSpec(memory_space=pltpu.SMEM),
            pl.BlockSpec((tq, LANES), lambda b, h, i: (b * nq + i, h)),
            pl.BlockSpec((seq, LANES), lambda b, h, i: (b, h)),
            pl.BlockSpec((seq, LANES), lambda b, h, i: (b, h)),
            pl.BlockSpec((1, LANES), lambda b, h, i: (0, 0)),
        ],
        out_specs=pl.BlockSpec((tq, LANES), lambda b, h, i: (b * nq + i, h)),
        out_shape=jax.ShapeDtypeStruct((t, DIFF_WIDTH), bf16),
        compiler_params=_cparams(("arbitrary", "arbitrary", "arbitrary")),
        name="diff_attn",
    )(scal, dq, dk, dv, g)


def _band_attn_kernel(sink_ref, q_ref, kp_ref, kc_ref, kn_ref, vp_ref, vc_ref, vn_ref, o_ref):
    n = pl.program_id(1)
    nb = pl.num_programs(1)
    k3 = jnp.concatenate([kp_ref[...], kc_ref[...], kn_ref[...]], axis=0)
    v3 = jnp.concatenate([vp_ref[...], vc_ref[...], vn_ref[...]], axis=0)
    r = lax.broadcasted_iota(jnp.int32, (BLOCK, 3 * BLOCK), 0)
    c = lax.broadcasted_iota(jnp.int32, (BLOCK, 3 * BLOCK), 1)
    valid = (c >= r) & (c <= r + 2 * BLOCK)
    valid = valid & ((c >= BLOCK) | (n > 0)) & ((c < 2 * BLOCK) | (n < nb - 1))
    lane = _lane_iota((BLOCK, LANES))
    valid4 = jnp.concatenate([valid] * SWA_GROUP, axis=0)
    grp = lax.broadcasted_iota(jnp.int32, (SWA_GROUP * BLOCK, 1), 0) // BLOCK
    outs = []
    for hk in range(SWA_KV_HEADS):
        qs = jnp.concatenate([q_ref[:, (hk * SWA_GROUP + g) * LANES:(hk * SWA_GROUP + g + 1) * LANES]
                              for g in range(SWA_GROUP)], axis=0)
        s = lax.dot_general(qs, k3, (((1,), (1,)), ((), ())), preferred_element_type=f32)
        s = jnp.where(valid4, s, -1e30)
        sink = jnp.zeros((SWA_GROUP * BLOCK, 1), f32)
        for g in range(SWA_GROUP):
            sink = jnp.where(grp == g, sink_ref[hk * SWA_GROUP + g], sink)
        m = jnp.maximum(jnp.max(s, axis=-1, keepdims=True), sink)
        e = jnp.exp(s - m)
        l = jnp.sum(e, axis=-1, keepdims=True) + jnp.exp(sink - m)
        p = (e * (1.0 / l)).astype(bf16)
        o = jnp.dot(p, v3, preferred_element_type=f32)
        outs.extend(o[g * BLOCK:(g + 1) * BLOCK] for g in range(SWA_GROUP))
    for pc in range(SWA_HEADS // 2):
        hk = (2 * pc) // SWA_GROUP
        left, right = outs[2 * pc], outs[2 * pc + 1]
        if hk == 1:
            left = pltpu.roll(left, HEAD_DIM, 1)
        else:
            right = pltpu.roll(right, HEAD_DIM, 1)
        o_ref[:, pc * LANES:(pc + 1) * LANES] = jnp.where(lane < HEAD_DIM, left, right).astype(bf16)


def _band_attn(sink, sqp, sk, sv, batch, seq):
    t = sqp.shape[0]
    nb = seq // BLOCK
    prev = lambda b, n: (b * nb + jnp.maximum(n - 1, 0), 0)
    cur = lambda b, n: (b * nb + n, 0)
    nxt = lambda b, n: (b * nb + jnp.minimum(n + 1, nb - 1), 0)
    kv = lambda f: pl.BlockSpec((BLOCK, LANES), f)
    return pl.pallas_call(
        _band_attn_kernel,
        grid=(batch, nb),
        in_specs=[pl.BlockSpec(memory_space=pltpu.SMEM), pl.BlockSpec((BLOCK, SWA_HEADS * LANES), cur),
                  kv(prev), kv(cur), kv(nxt), kv(prev), kv(cur), kv(nxt)],
        out_specs=pl.BlockSpec((BLOCK, SWA_WIDTH), cur),
        out_shape=jax.ShapeDtypeStruct((t, SWA_WIDTH), bf16),
        compiler_params=_cparams(("arbitrary", "arbitrary")),
        name="band_attn",
    )(sink, sqp, sk, sk, sk, sv, sv, sv)


def _attn_out_kernel(do_ref, so_ref, x_ref, wa_ref, wb_ref, g_ref, rhi_ref, rlo_ref, h_ref, hn_ref, aff_ref):
    h = x_ref[...] + jnp.dot(do_ref[...], wa_ref[...], preferred_element_type=f32) \
        + jnp.dot(so_ref[...], wb_ref[...], preferred_element_type=f32)
    h_ref[...] = h
    ms = jnp.mean(h * h, axis=-1, keepdims=True)
    hn = h * lax.rsqrt(ms + EPS) * g_ref[...]
    for k in range(ROW_TILES):
        hn_ref[:, k, :] = hn[:, k * LANES:(k + 1) * LANES]
    xh = hn.astype(bf16)
    xl = (hn - xh.astype(f32)).astype(bf16)
    rhi, rlo = rhi_ref[...], rlo_ref[...]
    logits = jnp.dot(xh, rhi, preferred_element_type=f32) + (jnp.dot(xh, rlo, preferred_element_type=f32)
                                                             + jnp.dot(xl, rhi, preferred_element_type=f32))
    lane = _lane_iota(logits.shape)
    live = lane < N_EXPERTS
    lm = jnp.where(live, logits, -1e30)
    m = jnp.max(lm, axis=-1, keepdims=True)
    e = jnp.where(live, jnp.exp(lm - m), 0.0)
    aff = e / jnp.sum(e, axis=-1, keepdims=True)
    aff_ref[...] = aff[:, 0:N_EXPERTS]


def _attn_out(do, so, x2, wa, wb, g, rhi, rlo, tb):
    t = x2.shape[0]
    row = lambda i: (i, 0)
    const = lambda i: (0, 0)
    return pl.pallas_call(
        _attn_out_kernel,
        grid=(t // tb,),
        in_specs=[pl.BlockSpec((tb, 512), row), pl.BlockSpec((tb, 512), row), pl.BlockSpec((tb, D_MODEL), row),
                  pl.BlockSpec((512, D_MODEL), const), pl.BlockSpec((512, D_MODEL), const),
                  pl.BlockSpec((1, D_MODEL), const), pl.BlockSpec((D_MODEL, LANES), const),
                  pl.BlockSpec((D_MODEL, LANES), const)],
        out_specs=[pl.BlockSpec((tb, D_MODEL), row), pl.BlockSpec((tb, ROW_TILES, LANES), lambda i: (i, 0, 0)),
                   pl.BlockSpec((tb, N_EXPERTS), row)],
        out_shape=[jax.ShapeDtypeStruct((t, D_MODEL), f32), jax.ShapeDtypeStruct((t, ROW_TILES, LANES), f32),
                   jax.ShapeDtypeStruct((t, N_EXPERTS), f32)],
        compiler_params=_cparams(("arbitrary",)),
        name="attn_out",
    )(do, so, x2, wa, wb, g, rhi, rlo)


def _class_sum(v):
    v = v + pltpu.roll(v, 64, 1)
    v = v + pltpu.roll(v, 32, 1)
    return v + pltpu.roll(v, 16, 1)


def _select_kernel(a_ref, o_ref, *, capacity):
    bits = pltpu.bitcast(a_ref[...], jnp.int32)

    def count_ge(cand):
        cnt = jnp.sum((bits >= cand[0:1, :]).astype(f32), axis=0, keepdims=True)
        return _class_sum(jnp.broadcast_to(cnt, (8, LANES)))

    def body(i, thr):
        cand = thr | jnp.left_shift(jnp.int32(1), 30 - i)
        return jnp.where(count_ge(cand) >= capacity, cand, thr)

    thr = lax.fori_loop(0, 31, body, jnp.zeros((8, LANES), jnp.int32))
    n_gt = count_ge(thr + 1)
    row = lax.broadcasted_iota(jnp.int32, (8, LANES), 0)
    o_ref[...] = jnp.where(row == 0, pltpu.bitcast(thr, f32), capacity - n_gt)


def _select(aff_dense, capacity):
    return pl.pallas_call(
        functools.partial(_select_kernel, capacity=float(capacity)),
        out_shape=jax.ShapeDtypeStruct((8, LANES), f32),
        compiler_params=pltpu.CompilerParams(vmem_limit_bytes=VMEM_LIMIT),
        name="select",
    )(aff_dense)


def _rank_kernel(aff_ref, thr_ref, need_ref, tok_ref, rank_ref, offs_out_ref, carry_ref, offs_ref):
    c = pl.program_id(0)

    @pl.when(c == 0)
    def _():
        tok_ref[...] = jnp.zeros_like(tok_ref)
        carry_ref[...] = jnp.zeros_like(carry_ref)
        for e in range(N_EXPERTS):
            offs_ref[e] = 0

    a = aff_ref[...]
    thr = thr_ref[...]
    gt = a > thr
    eq = a == thr
    ri = lax.broadcasted_iota(jnp.int32, (RANK_CHUNK, RANK_CHUNK), 0)
    ci = lax.broadcasted_iota(jnp.int32, (RANK_CHUNK, RANK_CHUNK), 1)
    lower = (ci < ri).astype(bf16)
    eqf = eq.astype(bf16)
    eq_rank = jnp.dot(lower, eqf, preferred_element_type=f32) + carry_ref[0:1, :]
    sel = gt | (eq & (eq_rank < need_ref[...]))
    self_ = sel.astype(bf16)
    rank = jnp.dot(lower, self_, preferred_element_type=f32) + carry_ref[1:2, :]
    rank_ref[...] = jnp.where(sel, rank, -1.0)
    n_sel = jnp.sum(sel.astype(f32), axis=0, keepdims=True)
    carry_ref[0:1, :] = carry_ref[0:1, :] + jnp.sum(eq.astype(f32), axis=0, keepdims=True)
    carry_ref[1:2, :] = carry_ref[1:2, :] + n_sel

    sub = lax.broadcasted_iota(jnp.int32, (8, RANK_CHUNK), 0)
    loc = lax.broadcasted_iota(jnp.int32, (8, RANK_CHUNK), 1)
    lhs = jnp.where(sub == 0, loc, jnp.where(sub == 1, 1, 0)).astype(f32).astype(bf16)
    slot = lax.broadcasted_iota(jnp.int32, (RANK_CHUNK, 3 * LANES), 1).astype(f32)
    base = (c * RANK_CHUNK).astype(f32)
    for e in range(N_EXPERTS):
        off = offs_ref[e]
        j0 = off // LANES
        rel = rank[:, e:e + 1] - (j0 * LANES).astype(f32)
        onehot = jnp.where((rel == slot) & sel[:, e:e + 1], 1.0, 0.0).astype(bf16)
        placed = jnp.dot(lhs, onehot, preferred_element_type=f32)
        row = lax.broadcasted_iota(jnp.int32, placed.shape, 0)
        filled = jnp.broadcast_to(placed[1:2, :], placed.shape)
        placed = jnp.where(row == 0, placed + base * filled, placed)
        for k in range(3):
            tok_ref[e, j0 + k] = tok_ref[e, j0 + k] + placed[:, k * LANES:(k + 1) * LANES]
        new_off = off + jnp.sum(n_sel[:, e:e + 1]).astype(jnp.int32)
        offs_ref[e] = new_off
        offs_out_ref[c * N_EXPERTS + e] = off
        offs_out_ref[(c + 1) * N_EXPERTS + e] = new_off


def _rank(aff16, thr, need, capacity):
    t = aff16.shape[0]
    n_chunks = t // RANK_CHUNK
    n_tiles = capacity // LANES + 3
    const2 = lambda c: (0, 0)
    return pl.pallas_call(
        _rank_kernel,
        grid=(n_chunks,),
        in_specs=[pl.BlockSpec((RANK_CHUNK, N_EXPERTS), lambda c: (c, 0)),
                  pl.BlockSpec((1, N_EXPERTS), const2), pl.BlockSpec((1, N_EXPERTS), const2)],
        out_specs=[pl.BlockSpec((N_EXPERTS, n_tiles, 8, LANES), lambda c: (0, 0, 0, 0)),
                   pl.BlockSpec((RANK_CHUNK, N_EXPERTS), lambda c: (c, 0)),
                   pl.BlockSpec(memory_space=pltpu.SMEM)],
        out_shape=[jax.ShapeDtypeStruct((N_EXPERTS, n_tiles, 8, LANES), f32),
                   jax.ShapeDtypeStruct((t, N_EXPERTS), f32),
                   jax.ShapeDtypeStruct(((n_chunks + 1) * N_EXPERTS,), jnp.int32)],
        scratch_shapes=[pltpu.VMEM((8, N_EXPERTS), f32), pltpu.SMEM((N_EXPERTS,), jnp.int32)],
        compiler_params=_cparams(("arbitrary",)),
        name="rank",
    )(aff16, thr, need)


DMA_UNROLL = 8


def _ffn_kernel(tok_ref, nxt_ref, rows_ref, wg_ref, wu_ref, wd_ref, o_ref, xbuf, sem):
    s = pl.program_id(0)
    n = pl.num_programs(0) - 1
    slot = s % 2
    tc = xbuf.shape[1]

    def start_rows(idx_ref, k):
        def body(r, carry):
            pltpu.make_async_copy(rows_ref.at[pl.ds(idx_ref[0, 0, r], 1)], xbuf.at[k, pl.ds(r, 1)], sem.at[k]).start()
            return carry
        lax.fori_loop(0, tc, body, 0, unroll=DMA_UNROLL)

    @pl.when(s == 0)
    def _():
        start_rows(tok_ref, slot)

    @pl.when(s + 1 < n)
    def _():
        start_rows(nxt_ref, 1 - slot)

    @pl.when(s < n)
    def _():
        pltpu.make_async_copy(rows_ref.at[pl.ds(0, tc)], xbuf.at[slot], sem.at[slot]).wait()
        xs = xbuf.at[slot]
        x = jnp.concatenate([xs[:, k, :] for k in range(ROW_TILES)], axis=1).astype(bf16)
        g = jnp.dot(x, wg_ref[0], preferred_element_type=f32)
        u = jnp.dot(x, wu_ref[0], preferred_element_type=f32)
        hmid = (g * jax.nn.sigmoid(g) * u).astype(bf16)
        o_ref[...] = jnp.dot(hmid, wd_ref[0], preferred_element_type=f32).astype(bf16)

    @pl.when(s == n)
    def _():
        o_ref[...] = jnp.zeros_like(o_ref)


def _ffn(tok, rows, wg, wu, wd, capacity, tc):
    nt = capacity // tc
    n = N_EXPERTS * nt
    expert = lambda s: (jnp.minimum(s // nt, N_EXPERTS - 1), 0, 0)
    return pl.pallas_call(
        _ffn_kernel,
        grid=(n + 1,),
        in_specs=[pl.BlockSpec((1, 1, tc), lambda s: (jnp.minimum(s, n - 1), 0, 0), memory_space=pltpu.SMEM),
                  pl.BlockSpec((1, 1, tc), lambda s: (jnp.minimum(s + 1, n - 1), 0, 0), memory_space=pltpu.SMEM),
                  pl.BlockSpec(memory_space=pl.ANY),
                  pl.BlockSpec((1, D_MODEL, EXPERT_FF), expert), pl.BlockSpec((1, D_MODEL, EXPERT_FF), expert),
                  pl.BlockSpec((1, EXPERT_FF, D_MODEL), expert)],
        out_specs=pl.BlockSpec((tc, D_MODEL), lambda s: (s, 0)),
        out_shape=jax.ShapeDtypeStruct(((n + 1) * tc, D_MODEL), bf16),
        scratch_shapes=[pltpu.VMEM((2, tc, ROW_TILES, LANES), f32), pltpu.SemaphoreType.DMA((2,))],
        compiler_params=_cparams(("arbitrary",)),
        name="ffn",
    )(tok, tok, rows, wg, wu, wd)


WIN = 64
ROW_ALIGN = 16


def _combine_kernel(offs_ref, h_ref, rank_ref, aff_ref, ye_ref, y_ref, win, sem, *, capacity):
    c = pl.program_id(0)
    starts, n_pass = [], jnp.int32(0)
    for e in range(N_EXPERTS):
        st = offs_ref[c * N_EXPERTS + e]
        cnt = offs_ref[(c + 1) * N_EXPERTS + e] - st
        a = (st // ROW_ALIGN) * ROW_ALIGN
        starts.append(a)
        n_pass = jnp.maximum(n_pass, jnp.where(cnt > 0, (st - a + cnt + WIN - 1) // WIN, 0))
    lane = _lane_iota((RANK_CHUNK, LANES))
    low = lane < WIN
    lane_f = lane.astype(f32)
    rank = rank_ref[...]
    aff = aff_ref[...]

    def one_pass(p, acc):
        def copy(e):
            src = jnp.minimum(e * capacity + starts[e] + p * WIN, ye_ref.shape[0] - WIN)
            return pltpu.make_async_copy(ye_ref.at[pl.ds(pl.multiple_of(src, ROW_ALIGN), WIN)],
                                         win.at[pl.ds(e * WIN, WIN)], sem.at[0])
        for e in range(N_EXPERTS):
            copy(e).start()
        tiles = []
        for i in range(N_EXPERTS // 2):
            ea, eb = 2 * i, 2 * i + 1
            ra = rank[:, ea:ea + 1] - (starts[ea] + p * WIN).astype(f32)
            rb = rank[:, eb:eb + 1] - (starts[eb] + p * WIN - WIN).astype(f32)
            hit = jnp.where(low, ra, rb) == lane_f
            gate = jnp.where(low, aff[:, ea:ea + 1], aff[:, eb:eb + 1])
            tiles.append(jnp.where(hit, gate, 0.0).astype(bf16))
        sel = jnp.concatenate(tiles, axis=1)
        for e in range(N_EXPERTS):
            copy(e).wait()
        return acc + jnp.dot(sel, win[...], preferred_element_type=f32)

    y_ref[...] = h_ref[...] + lax.fori_loop(0, n_pass, one_pass, jnp.zeros((RANK_CHUNK, D_MODEL), f32))


def _combine(offs, h, rank16, aff16, ye, capacity):
    t = h.shape[0]
    row = lambda c, offs: (c, 0)
    return pl.pallas_call(
        functools.partial(_combine_kernel, capacity=capacity),
        grid_spec=pltpu.PrefetchScalarGridSpec(
            num_scalar_prefetch=1,
            grid=(t // RANK_CHUNK,),
            in_specs=[pl.BlockSpec((RANK_CHUNK, D_MODEL), row), pl.BlockSpec((RANK_CHUNK, N_EXPERTS), row),
                      pl.BlockSpec((RANK_CHUNK, N_EXPERTS), row), pl.BlockSpec(memory_space=pl.ANY)],
            out_specs=pl.BlockSpec((RANK_CHUNK, D_MODEL), row),
            scratch_shapes=[pltpu.VMEM((N_EXPERTS * WIN, D_MODEL), bf16), pltpu.SemaphoreType.DMA((1,))]),
        out_shape=jax.ShapeDtypeStruct((t, D_MODEL), f32),
        compiler_params=_cparams(("arbitrary",)),
        name="combine",
    )(offs, h, rank16, aff16, ye)


def _rope_tables(seq):
    half = HEAD_DIM // 2
    inv = 1.0 / (ROPE_THETA ** (jnp.arange(0, HEAD_DIM, 2, dtype=f32) / HEAD_DIM))
    ang = jnp.arange(seq, dtype=f32)[:, None] * inv[None, :]
    cos = jnp.tile(jnp.cos(ang), (1, LANES // half))
    sign = jnp.where((jnp.arange(LANES) % HEAD_DIM) < half, -1.0, 1.0).astype(f32)
    sin = jnp.tile(jnp.sin(ang), (1, LANES // half)) * sign[None, :]
    return cos, sin


def _block_diag_mean(width):
    i = jnp.arange(width)
    return jnp.where((i[:, None] // HEAD_DIM) == (i[None, :] // HEAD_DIM), 1.0 / HEAD_DIM, 0.0).astype(bf16)


def _pick_tile(n, pref):
    while n % pref:
        pref //= 2
    return pref


def _trunk(x, p):
    batch, seq, _ = x.shape
    t = batch * seq
    capacity = CAPACITY_FACTOR * t // N_EXPERTS
    assert seq % BLOCK == 0 and t % RANK_CHUNK == 0 and capacity % LANES == 0
    tb = _pick_tile(seq, 512)
    tq = _pick_tile(seq, 256 if seq <= 2048 else 128)
    tc = _pick_tile(capacity, FFN_TILE)
    cos, sin = _rope_tables(seq)
    gm512, gm128 = _block_diag_mean(512), _block_diag_mean(128)
    x2 = x.reshape(t, D_MODEL)
    depth = p["w_in"].shape[0]
    for l in range(depth):
        lam_init = 0.8 - 0.6 * math.exp(-0.3 * l)
        lam = (jnp.exp(jnp.sum(p["lambda_q1"][l] * p["lambda_k1"][l]))
               - jnp.exp(jnp.sum(p["lambda_q2"][l] * p["lambda_k2"][l])) + lam_init)
        scal = jnp.stack([lam, jnp.asarray(1.0 - lam_init, f32)]).astype(f32)
        tile8 = lambda v: jnp.tile(v, 8)[None, :]
        dq, dk, dv, sqp, sk, sv = _attn_in(
            x2, p["attn_norm"][l][None, :], p["w_in_bf"][l], cos, sin,
            tile8(p["diff_q_norm"][l]), tile8(p["diff_k_norm"][l]), tile8(p["swa_q_norm"][l]),
            jnp.tile(p["swa_k_norm"][l], 2)[None, :], gm512, gm128, seq, tb)
        do = _diff_attn(scal, dq, dk, dv, p["diff_subln"][l][None, :], batch, seq, tq)
        so = _band_attn(p["swa_sink"][l], sqp, sk, sv, batch, seq)
        h, rows, aff16 = _attn_out(do, so, x2, p["w_out_bf"][l, :DIFF_WIDTH], p["w_out_bf"][l, DIFF_WIDTH:],
                                   p["ffn_norm"][l][None, :], p["r_hi"][l], p["r_lo"][l], tb)
        sel = _select(aff16.reshape(t // 8, LANES), capacity)
        tokbuf, rank16, offs = _rank(aff16, sel[0:1, :N_EXPERTS], sel[1:2, :N_EXPERTS], capacity)
        tok = tokbuf[:, :capacity // LANES, 0, :].reshape(N_EXPERTS * capacity // tc, 1, tc).astype(jnp.int32)
        ye = _ffn(tok, rows, p["w_gate_bf"][l], p["w_up_bf"][l], p["w_down_bf"][l], capacity, tc)
        x2 = _combine(offs, h, rank16, aff16, ye, capacity)
    return x2.reshape(batch, seq, D_MODEL)


def kernel(x_prompt, x_sample, attn_norm, w_in, diff_q_norm, diff_k_norm, lambda_q1, lambda_k1, lambda_q2, lambda_k2,
           diff_subln, swa_q_norm, swa_k_norm, swa_sink, w_out, ffn_norm, w_router, w_gate, w_up, w_down):
    r_pad = jnp.pad(w_router, ((0, 0), (0, 0), (0, LANES - N_EXPERTS)))
    r_hi = r_pad.astype(bf16)
    p = dict(attn_norm=attn_norm, diff_q_norm=diff_q_norm, diff_k_norm=diff_k_norm, lambda_q1=lambda_q1,
             lambda_k1=lambda_k1, lambda_q2=lambda_q2, lambda_k2=lambda_k2, diff_subln=diff_subln,
             swa_q_norm=swa_q_norm, swa_k_norm=swa_k_norm, swa_sink=swa_sink, ffn_norm=ffn_norm,
             w_in=w_in, w_in_bf=w_in.astype(bf16), w_out_bf=w_out.astype(bf16),
             r_hi=r_hi, r_lo=(r_pad - r_hi.astype(f32)).astype(bf16),
             w_gate_bf=w_gate.astype(bf16), w_up_bf=w_up.astype(bf16), w_down_bf=w_down.astype(bf16))
    return (_trunk(x_prompt, p), _trunk(x_sample, p))
```

```python
import functools
import math

import jax
import jax.numpy as jnp
from jax import lax
from jax.experimental import pallas as pl
from jax.experimental.pallas import tpu as pltpu

D_MODEL = 1024
HEAD_DIM = 64
DIFF_HEADS = 4
DIFF_VDIM = 128
DIFF_WIDTH = 512
SWA_HEADS = 8
SWA_KV_HEADS = 2
SWA_GROUP = 4
SWA_WIDTH = 512
BLOCK = 128
N_EXPERTS = 16
EXPERT_FF = 1024
CAPACITY_FACTOR = 2
ROPE_THETA = 10000.0
EPS = 1e-6
LOG2E = 1.4426950408889634
EXP_SHIFT_LIMIT = 60.0
DIFF_KEY_CHUNK = 512
IN_WIDTH = 2304
LANES = 128
ROW_TILES = D_MODEL // LANES
RANK_CHUNK = 256
FFN_TILE = 512
VMEM_LIMIT = 56 * 1024 * 1024

f32 = jnp.float32
bf16 = jnp.bfloat16


def _cparams(sem):
    return pltpu.CompilerParams(dimension_semantics=sem, vmem_limit_bytes=VMEM_LIMIT)


def _lane_iota(shape):
    return lax.broadcasted_iota(jnp.int32, shape, len(shape) - 1)


def _head_norm_rope(u, gain, gmat, cos, sin_signed):
    w = u.shape[1]
    ms = jnp.dot((u * u).astype(bf16), gmat, preferred_element_type=f32)
    un = u * lax.rsqrt(ms + EPS) * gain
    lane = _lane_iota(un.shape)
    first_half = (lane % HEAD_DIM) < (HEAD_DIM // 2)
    rot = jnp.where(first_half, pltpu.roll(un, w - HEAD_DIM // 2, 1), pltpu.roll(un, HEAD_DIM // 2, 1))
    return un * cos + rot * sin_signed


def _attn_in_kernel(x_ref, g_ref, w_ref, cos_ref, sin_ref, gdq_ref, gdk_ref, gsq_ref, gsk_ref, gm512_ref, gm128_ref,
                    dq_ref, dk_ref, dv_ref, sq_ref, sk_ref, sv_ref):
    x = x_ref[...]
    ms = jnp.mean(x * x, axis=-1, keepdims=True)
    xn = (x * lax.rsqrt(ms + EPS) * g_ref[...]).astype(bf16)
    z = jnp.dot(xn, w_ref[...], preferred_element_type=f32)
    cos1, sin1 = cos_ref[...], sin_ref[...]
    cos4 = jnp.concatenate([cos1] * 4, axis=1)
    sin4 = jnp.concatenate([sin1] * 4, axis=1)
    gm512 = gm512_ref[...]
    scale = 1.0 / math.sqrt(HEAD_DIM)
    dq = _head_norm_rope(z[:, 0:512], gdq_ref[...], gm512, cos4, sin4)
    dk = _head_norm_rope(z[:, 512:1024], gdk_ref[...], gm512, cos4, sin4)
    sq = _head_norm_rope(z[:, 1536:2048], gsq_ref[...], gm512, cos4, sin4)
    sk = _head_norm_rope(z[:, 2048:2176], gsk_ref[...], gm128_ref[...], cos1, sin1)
    dq_ref[...] = (dq * (scale * LOG2E)).astype(bf16)
    dk_ref[...] = dk.astype(bf16)
    dv_ref[...] = z[:, 1024:1536].astype(bf16)
    sk_ref[...] = sk.astype(bf16)
    sv_ref[...] = z[:, 2176:2304].astype(bf16)
    sq = sq * scale
    lane = _lane_iota((sq.shape[0], LANES))
    for j in range(SWA_HEADS):
        blk = sq[:, (j // 2) * LANES:(j // 2 + 1) * LANES]
        hk = j // SWA_GROUP
        if (j % 2) != hk:
            blk = pltpu.roll(blk, HEAD_DIM, 1)
        blk = jnp.where((lane // HEAD_DIM) == hk, blk, 0.0)
        sq_ref[:, j * LANES:(j + 1) * LANES] = blk.astype(bf16)


def _attn_in(x2, g, w_in, cos, sin, gdq, gdk, gsq, gsk, gm512, gm128, seq, tb):
    t = x2.shape[0]
    nb_seq = seq // tb
    row = lambda i: (i, 0)
    const = lambda i: (0, 0)
    outs = [jax.ShapeDtypeStruct((t, 512), bf16)] * 3 + [jax.ShapeDtypeStruct((t, 1024), bf16)] + \
           [jax.ShapeDtypeStruct((t, 128), bf16)] * 2
    return pl.pallas_call(
        _attn_in_kernel,
        grid=(t // tb,),
        in_specs=[
            pl.BlockSpec((tb, D_MODEL), row),
            pl.BlockSpec((1, D_MODEL), const),
            pl.BlockSpec((D_MODEL, IN_WIDTH), const),
            pl.BlockSpec((tb, LANES), lambda i: (i % nb_seq, 0)),
            pl.BlockSpec((tb, LANES), lambda i: (i % nb_seq, 0)),
            pl.BlockSpec((1, 512), const), pl.BlockSpec((1, 512), const), pl.BlockSpec((1, 512), const),
            pl.BlockSpec((1, 128), const),
            pl.BlockSpec((512, 512), const), pl.BlockSpec((128, 128), const),
        ],
        out_specs=[pl.BlockSpec((tb, 512), row)] * 3 + [pl.BlockSpec((tb, 1024), row)] + [pl.BlockSpec((tb, 128), row)] * 2,
        out_shape=outs,
        compiler_params=_cparams(("arbitrary",)),
        name="attn_in",
    )(x2, g, w_in, cos, sin, gdq, gdk, gsq, gsk, gm512, gm128)


def _diff_attn_kernel(scal_ref, q_ref, k_ref, v_ref, g_ref, o_ref, *, kc):
    lam = scal_ref[0]
    out_scale = scal_ref[1]
    bound = scal_ref[2]
    tq = q_ref.shape[0]
    n_heads = q_ref.shape[1] // LANES
    n_chunks = k_ref.shape[0] // kc
    lane = _lane_iota((tq, LANES))

    def stacked_q(h):
        q = q_ref[:, h * LANES:(h + 1) * LANES]
        zero = jnp.zeros_like(q)
        return jnp.concatenate([jnp.where(lane < HEAD_DIM, q, zero), jnp.where(lane >= HEAD_DIM, q, zero)], axis=0)

    def value_block(rows, h):
        vblk = v_ref[rows, h * LANES:(h + 1) * LANES]
        return jnp.concatenate([vblk, jnp.ones_like(vblk)], axis=1)

    def finish(ev, h):
        on = ev[:, 0:LANES] * (1.0 / ev[:, LANES:LANES + 1])
        o = on[:tq] - lam * on[tq:]
        ms = jnp.mean(o * o, axis=-1, keepdims=True)
        o_ref[:, h * LANES:(h + 1) * LANES] = ((o * lax.rsqrt(ms + EPS) * g_ref[...]) * out_scale).astype(bf16)

    @pl.when(bound <= EXP_SHIFT_LIMIT)
    def _():
        for h in range(n_heads):
            qs = stacked_q(h)
            ev = jnp.zeros((2 * tq, 2 * LANES), f32)
            for c in range(n_chunks):
                rows = slice(c * kc, (c + 1) * kc)
                s = lax.dot_general(qs, k_ref[rows, h * LANES:(h + 1) * LANES], (((1,), (1,)), ((), ())),
                                    preferred_element_type=f32)
                ev = ev + jnp.dot(jnp.exp2(s - bound).astype(bf16), value_block(rows, h), preferred_element_type=f32)
            finish(ev, h)

    @pl.when(bound > EXP_SHIFT_LIMIT)
    def _():
        for h in range(n_heads):
            qs = stacked_q(h)

            def chunk(c, carry, qs=qs, h=h):
                m, ev = carry
                rows = pl.ds(pl.multiple_of(c * kc, kc), kc)
                s = lax.dot_general(qs, k_ref[rows, h * LANES:(h + 1) * LANES], (((1,), (1,)), ((), ())),
                                    preferred_element_type=f32)
                m_new = jnp.maximum(m, jnp.max(s, axis=-1, keepdims=True))
                ev = jnp.exp2(m - m_new) * ev + jnp.dot(jnp.exp2(s - m_new).astype(bf16), value_block(rows, h),
                                                        preferred_element_type=f32)
                return m_new, ev

            init = (jnp.full((2 * tq, 1), -jnp.inf, f32), jnp.zeros((2 * tq, 2 * LANES), f32))
            finish(lax.fori_loop(0, n_chunks, chunk, init)[1], h)


def _diff_attn(scal, dq, dk, dv, g, batch, seq, tq, heads, kc):
    t = dq.shape[0]
    nq = seq // tq
    w = heads * LANES
    return pl.pallas_call(
        functools.partial(_diff_attn_kernel, kc=kc),
        grid=(batch, DIFF_HEADS // heads, nq),
        in_specs=[
            pl.BlockSpec(memory_space=pltpu.SMEM),
            pl.BlockSpec((tq, w), lambda b, h, i: (b * nq + i, h)),
            pl.BlockSpec((seq, w), lambda b, h, i: (b, h)),
            pl.BlockSpec((seq, w), lambda b, h, i: (b, h)),
            pl.BlockSpec((1, LANES), lambda b, h, i: (0, 0)),
        ],
        out_specs=pl.BlockSpec((tq, w), lambda b, h, i: (b * nq + i, h)),
        out_shape=jax.ShapeDtypeStruct((t, DIFF_WIDTH), bf16),
        compiler_params=_cparams(("arbitrary", "arbitrary", "arbitrary")),
        name="diff_attn",
    )(scal, dq, dk, dv, g)


def _band_attn_kernel(sink_ref, q_ref, kp_ref, kc_ref, kn_ref, vp_ref, vc_ref, vn_ref, o_ref):
    n = pl.program_id(1)
    nb = pl.num_programs(1)
    k3 = jnp.concatenate([kp_ref[...], kc_ref[...], kn_ref[...]], axis=0)
    v3 = jnp.concatenate([vp_ref[...], vc_ref[...], vn_ref[...]], axis=0)
    r = lax.broadcasted_iota(jnp.int32, (BLOCK, 3 * BLOCK), 0)
    c = lax.broadcasted_iota(jnp.int32, (BLOCK, 3 * BLOCK), 1)
    valid = (c >= r) & (c <= r + 2 * BLOCK)
    valid = valid & ((c >= BLOCK) | (n > 0)) & ((c < 2 * BLOCK) | (n < nb - 1))
    lane = _lane_iota((BLOCK, LANES))
    valid4 = jnp.concatenate([valid] * SWA_GROUP, axis=0)
    grp = lax.broadcasted_iota(jnp.int32, (SWA_GROUP * BLOCK, 1), 0) // BLOCK
    outs = []
    for hk in range(SWA_KV_HEADS):
        qs = jnp.concatenate([q_ref[:, (hk * SWA_GROUP + g) * LANES:(hk * SWA_GROUP + g + 1) * LANES]
                              for g in range(SWA_GROUP)], axis=0)
        s = lax.dot_general(qs, k3, (((1,), (1,)), ((), ())), preferred_element_type=f32)
        s = jnp.where(valid4, s, -1e30)
        sink = jnp.zeros((SWA_GROUP * BLOCK, 1), f32)
        for g in range(SWA_GROUP):
            sink = jnp.where(grp == g, sink_ref[hk * SWA_GROUP + g], sink)
        m = jnp.maximum(jnp.max(s, axis=-1, keepdims=True), sink)
        e = jnp.exp(s - m)
        l = jnp.sum(e, axis=-1, keepdims=True) + jnp.exp(sink - m)
        p = (e * (1.0 / l)).astype(bf16)
        o = jnp.dot(p, v3, preferred_element_type=f32)
        outs.extend(o[g * BLOCK:(g + 1) * BLOCK] for g in range(SWA_GROUP))
    for pc in range(SWA_HEADS // 2):
        hk = (2 * pc) // SWA_GROUP
        left, right = outs[2 * pc], outs[2 * pc + 1]
        if hk == 1:
            left = pltpu.roll(left, HEAD_DIM, 1)
        else:
            right = pltpu.roll(right, HEAD_DIM, 1)
        o_ref[:, pc * LANES:(pc + 1) * LANES] = jnp.where(lane < HEAD_DIM, left, right).astype(bf16)


def _band_attn(sink, sqp, sk, sv, batch, seq):
    t = sqp.shape[0]
    nb = seq // BLOCK
    prev = lambda b, n: (b * nb + jnp.maximum(n - 1, 0), 0)
    cur = lambda b, n: (b * nb + n, 0)
    nxt = lambda b, n: (b * nb + jnp.minimum(n + 1, nb - 1), 0)
    kv = lambda f: pl.BlockSpec((BLOCK, LANES), f)
    return pl.pallas_call(
        _band_attn_kernel,
        grid=(batch, nb),
        in_specs=[pl.BlockSpec(memory_space=pltpu.SMEM), pl.BlockSpec((BLOCK, SWA_HEADS * LANES), cur),
                  kv(prev), kv(cur), kv(nxt), kv(prev), kv(cur), kv(nxt)],
        out_specs=pl.BlockSpec((BLOCK, SWA_WIDTH), cur),
        out_shape=jax.ShapeDtypeStruct((t, SWA_WIDTH), bf16),
        compiler_params=_cparams(("arbitrary", "arbitrary")),
        name="band_attn",
    )(sink, sqp, sk, sk, sk, sv, sv, sv)


def _attn_out_kernel(do_ref, so_ref, x_ref, wa_ref, wb_ref, g_ref, rhi_ref, rlo_ref, h_ref, hn_ref, aff_ref):
    h = x_ref[...] + jnp.dot(do_ref[...], wa_ref[...], preferred_element_type=f32) \
        + jnp.dot(so_ref[...], wb_ref[...], preferred_element_type=f32)
    h_ref[...] = h
    ms = jnp.mean(h * h, axis=-1, keepdims=True)
    hn = h * lax.rsqrt(ms + EPS) * g_ref[...]
    for k in range(ROW_TILES):
        hn_ref[:, k, :] = hn[:, k * LANES:(k + 1) * LANES]
    xh = hn.astype(bf16)
    xl = (hn - xh.astype(f32)).astype(bf16)
    rhi, rlo = rhi_ref[...], rlo_ref[...]
    logits = jnp.dot(xh, rhi, preferred_element_type=f32) + (jnp.dot(xh, rlo, preferred_element_type=f32)
                                                             + jnp.dot(xl, rhi, preferred_element_type=f32))
    lane = _lane_iota(logits.shape)
    live = lane < N_EXPERTS
    lm = jnp.where(live, logits, -1e30)
    m = jnp.max(lm, axis=-1, keepdims=True)
    e = jnp.where(live, jnp.exp(lm - m), 0.0)
    aff = e / jnp.sum(e, axis=-1, keepdims=True)
    aff_ref[...] = aff[:, 0:N_EXPERTS]


def _attn_out(do, so, x2, wa, wb, g, rhi, rlo, tb):
    t = x2.shape[0]
    row = lambda i: (i, 0)
    const = lambda i: (0, 0)
    return pl.pallas_call(
        _attn_out_kernel,
        grid=(t // tb,),
        in_specs=[pl.BlockSpec((tb, 512), row), pl.BlockSpec((tb, 512), row), pl.BlockSpec((tb, D_MODEL), row),
                  pl.BlockSpec((512, D_MODEL), const), pl.BlockSpec((512, D_MODEL), const),
                  pl.BlockSpec((1, D_MODEL), const), pl.BlockSpec((D_MODEL, LANES), const),
                  pl.BlockSpec((D_MODEL, LANES), const)],
        out_specs=[pl.BlockSpec((tb, D_MODEL), row), pl.BlockSpec((tb, ROW_TILES, LANES), lambda i: (i, 0, 0)),
                   pl.BlockSpec((tb, N_EXPERTS), row)],
        out_shape=[jax.ShapeDtypeStruct((t, D_MODEL), f32), jax.ShapeDtypeStruct((t, ROW_TILES, LANES), f32),
                   jax.ShapeDtypeStruct((t, N_EXPERTS), f32)],
        compiler_params=_cparams(("arbitrary",)),
        name="attn_out",
    )(do, so, x2, wa, wb, g, rhi, rlo)


def _class_sum(v):
    v = v + pltpu.roll(v, 64, 1)
    v = v + pltpu.roll(v, 32, 1)
    return v + pltpu.roll(v, 16, 1)


def _select_kernel(a_ref, o_ref, *, capacity):
    bits = pltpu.bitcast(a_ref[...], jnp.int32)

    def count_ge(cand):
        cnt = jnp.sum((bits >= cand[0:1, :]).astype(f32), axis=0, keepdims=True)
        return _class_sum(jnp.broadcast_to(cnt, (8, LANES)))

    def body(i, thr):
        cand = thr | jnp.left_shift(jnp.int32(1), 30 - i)
        return jnp.where(count_ge(cand) >= capacity, cand, thr)

    thr = lax.fori_loop(0, 31, body, jnp.zeros((8, LANES), jnp.int32))
    n_gt = count_ge(thr + 1)
    row = lax.broadcasted_iota(jnp.int32, (8, LANES), 0)
    o_ref[...] = jnp.where(row == 0, pltpu.bitcast(thr, f32), capacity - n_gt)


def _select(aff_dense, capacity):
    return pl.pallas_call(
        functools.partial(_select_kernel, capacity=float(capacity)),
        out_shape=jax.ShapeDtypeStruct((8, LANES), f32),
        compiler_params=pltpu.CompilerParams(vmem_limit_bytes=VMEM_LIMIT),
        name="select",
    )(aff_dense)


def _rank_kernel(aff_ref, thr_ref, need_ref, tok_ref, rank_ref, offs_out_ref, carry_ref, offs_ref):
    c = pl.program_id(0)

    @pl.when(c == 0)
    def _():
        tok_ref[...] = jnp.zeros_like(tok_ref)
        carry_ref[...] = jnp.zeros_like(carry_ref)
        for e in range(N_EXPERTS):
            offs_ref[e] = 0

    a = aff_ref[...]
    thr = thr_ref[...]
    gt = a > thr
    eq = a == thr
    ri = lax.broadcasted_iota(jnp.int32, (RANK_CHUNK, RANK_CHUNK), 0)
    ci = lax.broadcasted_iota(jnp.int32, (RANK_CHUNK, RANK_CHUNK), 1)
    lower = (ci < ri).astype(bf16)
    eqf = eq.astype(bf16)
    eq_rank = jnp.dot(lower, eqf, preferred_element_type=f32) + carry_ref[0:1, :]
    sel = gt | (eq & (eq_rank < need_ref[...]))
    self_ = sel.astype(bf16)
    rank = jnp.dot(lower, self_, preferred_element_type=f32) + carry_ref[1:2, :]
    rank_ref[...] = jnp.where(sel, rank, -1.0)
    n_sel = jnp.sum(sel.astype(f32), axis=0, keepdims=True)
    carry_ref[0:1, :] = carry_ref[0:1, :] + jnp.sum(eq.astype(f32), axis=0, keepdims=True)
    carry_ref[1:2, :] = carry_ref[1:2, :] + n_sel

    sub = lax.broadcasted_iota(jnp.int32, (8, RANK_CHUNK), 0)
    loc = lax.broadcasted_iota(jnp.int32, (8, RANK_CHUNK), 1)
    lhs = jnp.where(sub == 0, loc, jnp.where(sub == 1, 1, 0)).astype(f32).astype(bf16)
    slot = lax.broadcasted_iota(jnp.int32, (RANK_CHUNK, 3 * LANES), 1).astype(f32)
    base = (c * RANK_CHUNK).astype(f32)
    for e in range(N_EXPERTS):
        off = offs_ref[e]
        j0 = off // LANES
        rel = rank[:, e:e + 1] - (j0 * LANES).astype(f32)
        onehot = jnp.where((rel == slot) & sel[:, e:e + 1], 1.0, 0.0).astype(bf16)
        placed = jnp.dot(lhs, onehot, preferred_element_type=f32)
        row = lax.broadcasted_iota(jnp.int32, placed.shape, 0)
        filled = jnp.broadcast_to(placed[1:2, :], placed.shape)
        placed = jnp.where(row == 0, placed + base * filled, placed)
        for k in range(3):
            tok_ref[e, j0 + k] = tok_ref[e, j0 + k] + placed[:, k * LANES:(k + 1) * LANES]
        new_off = off + jnp.sum(n_sel[:, e:e + 1]).astype(jnp.int32)
        offs_ref[e] = new_off
        offs_out_ref[c * N_EXPERTS + e] = off
        offs_out_ref[(c + 1) * N_EXPERTS + e] = new_off


def _rank(aff16, thr, need, capacity):
    t = aff16.shape[0]
    n_chunks = t // RANK_CHUNK
    n_tiles = capacity // LANES + 3
    const2 = lambda c: (0, 0)
    return pl.pallas_call(
        _rank_kernel,
        grid=(n_chunks,),
        in_specs=[pl.BlockSpec((RANK_CHUNK, N_EXPERTS), lambda c: (c, 0)),
                  pl.BlockSpec((1, N_EXPERTS), const2), pl.BlockSpec((1, N_EXPERTS), const2)],
        out_specs=[pl.BlockSpec((N_EXPERTS, n_tiles, 8, LANES), lambda c: (0, 0, 0, 0)),
                   pl.BlockSpec((RANK_CHUNK, N_EXPERTS), lambda c: (c, 0)),
                   pl.BlockSpec(memory_space=pltpu.SMEM)],
        out_shape=[jax.ShapeDtypeStruct((N_EXPERTS, n_tiles, 8, LANES), f32),
                   jax.ShapeDtypeStruct((t, N_EXPERTS), f32),
                   jax.ShapeDtypeStruct(((n_chunks + 1) * N_EXPERTS,), jnp.int32)],
        scratch_shapes=[pltpu.VMEM((8, N_EXPERTS), f32), pltpu.SMEM((N_EXPERTS,), jnp.int32)],
        compiler_params=_cparams(("arbitrary",)),
        name="rank",
    )(aff16, thr, need)


DMA_UNROLL = 8


def _ffn_kernel(tok_ref, nxt_ref, rows_ref, wg_ref, wu_ref, wd_ref, o_ref, xbuf, sem):
    s = pl.program_id(0)
    n = pl.num_programs(0) - 1
    slot = s % 2
    tc = xbuf.shape[1]

    def row_copy(idx_ref, k, r):
        return pltpu.make_async_copy(rows_ref.at[pl.ds(idx_ref[0, 0, r], 1)], xbuf.at[k, pl.ds(r, 1)], sem.at[k])

    def wait_rows(k):
        pltpu.make_async_copy(rows_ref.at[pl.ds(0, tc)], xbuf.at[k], sem.at[k]).wait()

    @pl.when(s == 0)
    def _():
        def body(r, carry):
            row_copy(tok_ref, slot, r).start()
            return carry
        lax.fori_loop(0, tc, body, 0, unroll=DMA_UNROLL)

    @pl.when(s < n)
    def _():
        for r in range(tc):
            row_copy(nxt_ref, 1 - slot, r).start()
        wait_rows(slot)
        xs = xbuf.at[slot]
        x = jnp.concatenate([xs[:, k, :] for k in range(ROW_TILES)], axis=1).astype(bf16)
        g = jnp.dot(x, wg_ref[0], preferred_element_type=f32)
        u = jnp.dot(x, wu_ref[0], preferred_element_type=f32)
        hmid = (g * jax.nn.sigmoid(g) * u).astype(bf16)
        o_ref[...] = jnp.dot(hmid, wd_ref[0], preferred_element_type=f32).astype(bf16)

    @pl.when(s == n)
    def _():
        wait_rows(slot)
        o_ref[...] = jnp.zeros_like(o_ref)


def _ffn(tok, rows, wg, wu, wd, capacity, tc):
    nt = capacity // tc
    n = N_EXPERTS * nt
    expert = lambda s: (jnp.minimum(s // nt, N_EXPERTS - 1), 0, 0)
    return pl.pallas_call(
        _ffn_kernel,
        grid=(n + 1,),
        in_specs=[pl.BlockSpec((1, 1, tc), lambda s: (jnp.minimum(s, n - 1), 0, 0), memory_space=pltpu.SMEM),
                  pl.BlockSpec((1, 1, tc), lambda s: (jnp.minimum(s + 1, n - 1), 0, 0), memory_space=pltpu.SMEM),
                  pl.BlockSpec(memory_space=pl.ANY),
                  pl.BlockSpec((1, D_MODEL, EXPERT_FF), expert), pl.BlockSpec((1, D_MODEL, EXPERT_FF), expert),
                  pl.BlockSpec((1, EXPERT_FF, D_MODEL), expert)],
        out_specs=pl.BlockSpec((tc, D_MODEL), lambda s: (s, 0)),
        out_shape=jax.ShapeDtypeStruct(((n + 1) * tc, D_MODEL), bf16),
        scratch_shapes=[pltpu.VMEM((2, tc, ROW_TILES, LANES), f32), pltpu.SemaphoreType.DMA((2,))],
        compiler_params=_cparams(("arbitrary",)),
        name="ffn",
    )(tok, tok, rows, wg, wu, wd)


WIN = 64
ROW_ALIGN = 16


def _combine_kernel(offs_ref, h_ref, rank_ref, aff_ref, ye_ref, y_ref, win, sem, *, capacity):
    c = pl.program_id(0)
    n_chunks = pl.num_programs(0)
    slot = c % 2
    width = N_EXPERTS * WIN

    def window_starts(chunk):
        return [(offs_ref[chunk * N_EXPERTS + e] // ROW_ALIGN) * ROW_ALIGN for e in range(N_EXPERTS)]

    def copy(e, start, p, k):
        src = jnp.minimum(e * capacity + start + p * WIN, ye_ref.shape[0] - WIN)
        return pltpu.make_async_copy(ye_ref.at[pl.ds(pl.multiple_of(src, ROW_ALIGN), WIN)],
                                     win.at[k, pl.ds(e * WIN, WIN)], sem.at[k])

    starts = window_starts(c)
    n_pass = jnp.int32(0)
    for e in range(N_EXPERTS):
        st = offs_ref[c * N_EXPERTS + e]
        cnt = offs_ref[(c + 1) * N_EXPERTS + e] - st
        n_pass = jnp.maximum(n_pass, jnp.where(cnt > 0, (st - starts[e] + cnt + WIN - 1) // WIN, 0))

    @pl.when(c == 0)
    def _():
        for e in range(N_EXPERTS):
            copy(e, starts[e], 0, slot).start()

    @pl.when(c + 1 < n_chunks)
    def _():
        nxt = window_starts(c + 1)
        for e in range(N_EXPERTS):
            copy(e, nxt[e], 0, 1 - slot).start()

    col = lax.broadcasted_iota(jnp.int32, (N_EXPERTS, width), 1)
    row = lax.broadcasted_iota(jnp.int32, (N_EXPERTS, width), 0)
    spread = jnp.where((col // WIN) == row, 1.0, 0.0).astype(bf16)
    rank = rank_ref[...]
    hi = jnp.floor(rank * (1.0 / LANES))
    lo = rank - hi * LANES
    rank_rep = LANES * jnp.dot(hi.astype(bf16), spread, preferred_element_type=f32) \
        + jnp.dot(lo.astype(bf16), spread, preferred_element_type=f32)
    gate_rep = jnp.dot(aff_ref[...].astype(bf16), spread, preferred_element_type=f32)
    col1 = lax.broadcasted_iota(jnp.int32, (1, width), 1)
    base = jnp.zeros((1, width), jnp.int32)
    for e in range(N_EXPERTS):
        base = jnp.where((col1 // WIN) == e, starts[e], base)
    base = (base + col1 % WIN).astype(f32)

    def scatter_matmul(p, k):
        hit = rank_rep == base + (p * WIN).astype(f32)
        sel = jnp.where(hit, gate_rep, 0.0).astype(bf16)
        return jnp.dot(sel, win[k], preferred_element_type=f32)

    def wait_windows(k):
        for e in range(N_EXPERTS):
            copy(e, starts[e], 0, k).wait()

    wait_windows(slot)
    acc = scatter_matmul(jnp.int32(0), slot)

    def later_pass(p, acc):
        for e in range(N_EXPERTS):
            copy(e, starts[e], p, slot).start()
        wait_windows(slot)
        return acc + scatter_matmul(p, slot)

    y_ref[...] = h_ref[...] + lax.fori_loop(1, n_pass, later_pass, acc)


def _combine(offs, h, rank16, aff16, ye, capacity):
    t = h.shape[0]
    row = lambda c, offs: (c, 0)
    return pl.pallas_call(
        functools.partial(_combine_kernel, capacity=capacity),
        grid_spec=pltpu.PrefetchScalarGridSpec(
            num_scalar_prefetch=1,
            grid=(t // RANK_CHUNK,),
            in_specs=[pl.BlockSpec((RANK_CHUNK, D_MODEL), row), pl.BlockSpec((RANK_CHUNK, N_EXPERTS), row),
                      pl.BlockSpec((RANK_CHUNK, N_EXPERTS), row), pl.BlockSpec(memory_space=pl.ANY)],
            out_specs=pl.BlockSpec((RANK_CHUNK, D_MODEL), row),
            scratch_shapes=[pltpu.VMEM((2, N_EXPERTS * WIN, D_MODEL), bf16), pltpu.SemaphoreType.DMA((2,))]),
        out_shape=jax.ShapeDtypeStruct((t, D_MODEL), f32),
        compiler_params=_cparams(("arbitrary",)),
        name="combine",
    )(offs, h, rank16, aff16, ye)


def _rope_tables(seq):
    half = HEAD_DIM // 2
    inv = 1.0 / (ROPE_THETA ** (jnp.arange(0, HEAD_DIM, 2, dtype=f32) / HEAD_DIM))
    ang = jnp.arange(seq, dtype=f32)[:, None] * inv[None, :]
    cos = jnp.tile(jnp.cos(ang), (1, LANES // half))
    sign = jnp.where((jnp.arange(LANES) % HEAD_DIM) < half, -1.0, 1.0).astype(f32)
    sin = jnp.tile(jnp.sin(ang), (1, LANES // half)) * sign[None, :]
    return cos, sin


def _block_diag_mean(width):
    i = jnp.arange(width)
    return jnp.where((i[:, None] // HEAD_DIM) == (i[None, :] // HEAD_DIM), 1.0 / HEAD_DIM, 0.0).astype(bf16)


def _pick_tile(n, pref):
    while n % pref:
        pref //= 2
    return pref


def _trunk(x, p):
    batch, seq, _ = x.shape
    t = batch * seq
    capacity = CAPACITY_FACTOR * t // N_EXPERTS
    assert seq % BLOCK == 0 and t % RANK_CHUNK == 0 and capacity % LANES == 0
    tb = _pick_tile(seq, 512)
    tq = _pick_tile(seq, 256)
    kc = _pick_tile(seq, DIFF_KEY_CHUNK)
    heads = DIFF_HEADS if seq <= 2048 else DIFF_HEADS // 2
    tc = _pick_tile(capacity, FFN_TILE)
    cos, sin = _rope_tables(seq)
    gm512, gm128 = _block_diag_mean(512), _block_diag_mean(128)
    x2 = x.reshape(t, D_MODEL)
    depth = p["w_in"].shape[0]
    for l in range(depth):
        lam_init = 0.8 - 0.6 * math.exp(-0.3 * l)
        lam = (jnp.exp(jnp.sum(p["lambda_q1"][l] * p["lambda_k1"][l]))
               - jnp.exp(jnp.sum(p["lambda_q2"][l] * p["lambda_k2"][l])) + lam_init)
        bound = (8.0 * LOG2E * (1.0 + 2.0 ** -6)) * jnp.max(jnp.abs(p["diff_q_norm"][l])) \
            * jnp.max(jnp.abs(p["diff_k_norm"][l]))
        scal = jnp.stack([lam, jnp.asarray(1.0 - lam_init, f32), bound]).astype(f32)
        tile8 = lambda v: jnp.tile(v, 8)[None, :]
        dq, dk, dv, sqp, sk, sv = _attn_in(
            x2, p["attn_norm"][l][None, :], p["w_in_bf"][l], cos, sin,
            tile8(p["diff_q_norm"][l]), tile8(p["diff_k_norm"][l]), tile8(p["swa_q_norm"][l]),
            jnp.tile(p["swa_k_norm"][l], 2)[None, :], gm512, gm128, seq, tb)
        do = _diff_attn(scal, dq, dk, dv, p["diff_subln"][l][None, :], batch, seq, tq, heads, kc)
        so = _band_attn(p["swa_sink"][l], sqp, sk, sv, batch, seq)
        h, rows, aff16 = _attn_out(do, so, x2, p["w_out_bf"][l, :DIFF_WIDTH], p["w_out_bf"][l, DIFF_WIDTH:],
                                   p["ffn_norm"][l][None, :], p["r_hi"][l], p["r_lo"][l], tb)
        sel = _select(aff16.reshape(t // 8, LANES), capacity)
        tokbuf, rank16, offs = _rank(aff16, sel[0:1, :N_EXPERTS], sel[1:2, :N_EXPERTS], capacity)
        tok = tokbuf[:, :capacity // LANES, 0, :].reshape(N_EXPERTS * capacity // tc, 1, tc).astype(jnp.int32)
        ye = _ffn(tok, rows, p["w_gate_bf"][l], p["w_up_bf"][l], p["w_down_bf"][l], capacity, tc)
        x2 = _combine(offs, h, rank16, aff16, ye, capacity)
    return x2.reshape(batch, seq, D_MODEL)


def kernel(x_prompt, x_sample, attn_norm, w_in, diff_q_norm, diff_k_norm, lambda_q1, lambda_k1, lambda_q2, lambda_k2,
           diff_subln, swa_q_norm, swa_k_norm, swa_sink, w_out, ffn_norm, w_router, w_gate, w_up, w_down):
    r_pad = jnp.pad(w_router, ((0, 0), (0, 0), (0, LANES - N_EXPERTS)))
    r_hi = r_pad.astype(bf16)
    p = dict(attn_norm=attn_norm, diff_q_norm=diff_q_norm, diff_k_norm=diff_k_norm, lambda_q1=lambda_q1,
             lambda_k1=lambda_k1, lambda_q2=lambda_q2, lambda_k2=lambda_k2, diff_subln=diff_subln,
             swa_q_norm=swa_q_norm, swa_k_norm=swa_k_norm, swa_sink=swa_sink, ffn_norm=ffn_norm,
             w_in=w_in, w_in_bf=w_in.astype(bf16), w_out_bf=w_out.astype(bf16),
             r_hi=r_hi, r_lo=(r_pad - r_hi.astype(f32)).astype(bf16),
             w_gate_bf=w_gate.astype(bf16), w_up_bf=w_up.astype(bf16), w_down_bf=w_down.astype(bf16))
    return (_trunk(x_prompt, p), _trunk(x_sample, p))
```

```python
import functools
import math

import jax
import jax.numpy as jnp
from jax import lax
from jax.experimental import pallas as pl
from jax.experimental.pallas import tpu as pltpu

D_MODEL = 1024
HEAD_DIM = 64
DIFF_HEADS = 4
DIFF_VDIM = 128
DIFF_WIDTH = 512
SWA_HEADS = 8
SWA_KV_HEADS = 2
SWA_GROUP = 4
SWA_WIDTH = 512
BLOCK = 128
N_EXPERTS = 16
EXPERT_FF = 1024
CAPACITY_FACTOR = 2
ROPE_THETA = 10000.0
EPS = 1e-6
LOG2E = 1.4426950408889634
EXP_SHIFT_LIMIT = 60.0
DIFF_KEY_CHUNK = 512
IN_WIDTH = 2304
LANES = 128
ROW_TILES = D_MODEL // LANES
RANK_CHUNK = 256
FFN_TILE = 512
VMEM_LIMIT = 56 * 1024 * 1024

f32 = jnp.float32
bf16 = jnp.bfloat16


def _cparams(sem):
    return pltpu.CompilerParams(dimension_semantics=sem, vmem_limit_bytes=VMEM_LIMIT)


def _lane_iota(shape):
    return lax.broadcasted_iota(jnp.int32, shape, len(shape) - 1)


def _head_norm_rope(u, gain, gmat, cos, sin_signed):
    w = u.shape[1]
    ms = jnp.dot((u * u).astype(bf16), gmat, preferred_element_type=f32)
    un = u * lax.rsqrt(ms + EPS) * gain
    lane = _lane_iota(un.shape)
    first_half = (lane % HEAD_DIM) < (HEAD_DIM // 2)
    rot = jnp.where(first_half, pltpu.roll(un, w - HEAD_DIM // 2, 1), pltpu.roll(un, HEAD_DIM // 2, 1))
    return un * cos + rot * sin_signed


def _attn_in_kernel(x_ref, g_ref, w_ref, cos_ref, sin_ref, gdq_ref, gdk_ref, gsq_ref, gsk_ref, gm512_ref, gm128_ref,
                    dq_ref, dk_ref, dv_ref, sq_ref, sk_ref, sv_ref):
    x = x_ref[...]
    ms = jnp.mean(x * x, axis=-1, keepdims=True)
    xn = (x * lax.rsqrt(ms + EPS) * g_ref[...]).astype(bf16)
    z = jnp.dot(xn, w_ref[...], preferred_element_type=f32)
    cos1, sin1 = cos_ref[...], sin_ref[...]
    cos4 = jnp.concatenate([cos1] * 4, axis=1)
    sin4 = jnp.concatenate([sin1] * 4, axis=1)
    gm512 = gm512_ref[...]
    scale = 1.0 / math.sqrt(HEAD_DIM)
    dq = _head_norm_rope(z[:, 0:512], gdq_ref[...], gm512, cos4, sin4)
    dk = _head_norm_rope(z[:, 512:1024], gdk_ref[...], gm512, cos4, sin4)
    sq = _head_norm_rope(z[:, 1536:2048], gsq_ref[...], gm512, cos4, sin4)
    sk = _head_norm_rope(z[:, 2048:2176], gsk_ref[...], gm128_ref[...], cos1, sin1)
    dq_ref[...] = (dq * (scale * LOG2E)).astype(bf16)
    dk_ref[...] = dk.astype(bf16)
    dv_ref[...] = z[:, 1024:1536].astype(bf16)
    sk_ref[...] = sk.astype(bf16)
    sv_ref[...] = z[:, 2176:2304].astype(bf16)
    sq = sq * scale
    lane = _lane_iota((sq.shape[0], LANES))
    for j in range(SWA_HEADS):
        blk = sq[:, (j // 2) * LANES:(j // 2 + 1) * LANES]
        hk = j // SWA_GROUP
        if (j % 2) != hk:
            blk = pltpu.roll(blk, HEAD_DIM, 1)
        blk = jnp.where((lane // HEAD_DIM) == hk, blk, 0.0)
        sq_ref[:, j * LANES:(j + 1) * LANES] = blk.astype(bf16)


def _attn_in(x2, g, w_in, cos, sin, gdq, gdk, gsq, gsk, gm512, gm128, seq, tb):
    t = x2.shape[0]
    nb_seq = seq // tb
    row = lambda i: (i, 0)
    const = lambda i: (0, 0)
    outs = [jax.ShapeDtypeStruct((t, 512), bf16)] * 3 + [jax.ShapeDtypeStruct((t, 1024), bf16)] + \
           [jax.ShapeDtypeStruct((t, 128), bf16)] * 2
    return pl.pallas_call(
        _attn_in_kernel,
        grid=(t // tb,),
        in_specs=[
            pl.BlockSpec((tb, D_MODEL), row),
            pl.BlockSpec((1, D_MODEL), const),
            pl.BlockSpec((D_MODEL, IN_WIDTH), const),
            pl.BlockSpec((tb, LANES), lambda i: (i % nb_seq, 0)),
            pl.BlockSpec((tb, LANES), lambda i: (i % nb_seq, 0)),
            pl.BlockSpec((1, 512), const), pl.BlockSpec((1, 512), const), pl.BlockSpec((1, 512), const),
            pl.BlockSpec((1, 128), const),
            pl.BlockSpec((512, 512), const), pl.BlockSpec((128, 128), const),
        ],
        out_specs=[pl.BlockSpec((tb, 512), row)] * 3 + [pl.BlockSpec((tb, 1024), row)] + [pl.BlockSpec((tb, 128), row)] * 2,
        out_shape=outs,
        compiler_params=_cparams(("arbitrary",)),
        name="attn_in",
    )(x2, g, w_in, cos, sin, gdq, gdk, gsq, gsk, gm512, gm128)


def _diff_attn_kernel(scal_ref, q_ref, k_ref, v_ref, g_ref, o_ref, *, kc):
    lam = scal_ref[0]
    out_scale = scal_ref[1]
    bound = scal_ref[2]
    tq = q_ref.shape[0]
    n_heads = q_ref.shape[1] // LANES
    n_chunks = k_ref.shape[0] // kc
    lane = _lane_iota((tq, LANES))

    def stacked_q(h):
        q = q_ref[:, h * LANES:(h + 1) * LANES]
        zero = jnp.zeros_like(q)
        return jnp.concatenate([jnp.where(lane < HEAD_DIM, q, zero), jnp.where(lane >= HEAD_DIM, q, zero)], axis=0)

    def value_block(rows, h):
        vblk = v_ref[rows, h * LANES:(h + 1) * LANES]
        return jnp.concatenate([vblk, jnp.ones_like(vblk)], axis=1)

    def finish(ev, h):
        on = ev[:, 0:LANES] * (1.0 / ev[:, LANES:LANES + 1])
        o = on[:tq] - lam * on[tq:]
        ms = jnp.mean(o * o, axis=-1, keepdims=True)
        o_ref[:, h * LANES:(h + 1) * LANES] = ((o * lax.rsqrt(ms + EPS) * g_ref[...]) * out_scale).astype(bf16)

    @pl.when(bound <= EXP_SHIFT_LIMIT)
    def _():
        for h in range(n_heads):
            qs = stacked_q(h)
            ev = jnp.zeros((2 * tq, 2 * LANES), f32)
            for c in range(n_chunks):
                rows = slice(c * kc, (c + 1) * kc)
                s = lax.dot_general(qs, k_ref[rows, h * LANES:(h + 1) * LANES], (((1,), (1,)), ((), ())),
                                    preferred_element_type=f32)
                ev = ev + jnp.dot(jnp.exp2(s - bound).astype(bf16), value_block(rows, h), preferred_element_type=f32)
            finish(ev, h)

    @pl.when(bound > EXP_SHIFT_LIMIT)
    def _():
        for h in range(n_heads):
            qs = stacked_q(h)

            def chunk(c, carry, qs=qs, h=h):
                m, ev = carry
                rows = pl.ds(pl.multiple_of(c * kc, kc), kc)
                s = lax.dot_general(qs, k_ref[rows, h * LANES:(h + 1) * LANES], (((1,), (1,)), ((), ())),
                                    preferred_element_type=f32)
                m_new = jnp.maximum(m, jnp.max(s, axis=-1, keepdims=True))
                ev = jnp.exp2(m - m_new) * ev + jnp.dot(jnp.exp2(s - m_new).astype(bf16), value_block(rows, h),
                                                        preferred_element_type=f32)
                return m_new, ev

            init = (jnp.full((2 * tq, 1), -jnp.inf, f32), jnp.zeros((2 * tq, 2 * LANES), f32))
            finish(lax.fori_loop(0, n_chunks, chunk, init)[1], h)


def _diff_attn(scal, dq, dk, dv, g, batch, seq, tq, heads, kc):
    t = dq.shape[0]
    nq = seq // tq
    w = heads * LANES
    return pl.pallas_call(
        functools.partial(_diff_attn_kernel, kc=kc),
        grid=(batch, DIFF_HEADS // heads, nq),
        in_specs=[
            pl.BlockSpec(memory_space=pltpu.SMEM),
            pl.BlockSpec((tq, w), lambda b, h, i: (b * nq + i, h)),
            pl.BlockSpec((seq, w), lambda b, h, i: (b, h)),
            pl.BlockSpec((seq, w), lambda b, h, i: (b, h)),
            pl.BlockSpec((1, LANES), lambda b, h, i: (0, 0)),
        ],
        out_specs=pl.BlockSpec((tq, w), lambda b, h, i: (b * nq + i, h)),
        out_shape=jax.ShapeDtypeStruct((t, DIFF_WIDTH), bf16),
        compiler_params=_cparams(("arbitrary", "arbitrary", "arbitrary")),
        name="diff_attn",
    )(scal, dq, dk, dv, g)


def _band_attn_kernel(sink_ref, q_ref, kp_ref, kc_ref, kn_ref, vp_ref, vc_ref, vn_ref, o_ref):
    n = pl.program_id(1)
    nb = pl.num_programs(1)
    k3 = jnp.concatenate([kp_ref[...], kc_ref[...], kn_ref[...]], axis=0)
    v3 = jnp.concatenate([vp_ref[...], vc_ref[...], vn_ref[...]], axis=0)
    r = lax.broadcasted_iota(jnp.int32, (BLOCK, 3 * BLOCK), 0)
    c = lax.broadcasted_iota(jnp.int32, (BLOCK, 3 * BLOCK), 1)
    valid = (c >= r) & (c <= r + 2 * BLOCK)
    valid = valid & ((c >= BLOCK) | (n > 0)) & ((c < 2 * BLOCK) | (n < nb - 1))
    lane = _lane_iota((BLOCK, LANES))
    valid4 = jnp.concatenate([valid] * SWA_GROUP, axis=0)
    grp = lax.broadcasted_iota(jnp.int32, (SWA_GROUP * BLOCK, 1), 0) // BLOCK
    outs = []
    for hk in range(SWA_KV_HEADS):
        qs = jnp.concatenate([q_ref[:, (hk * SWA_GROUP + g) * LANES:(hk * SWA_GROUP + g + 1) * LANES]
                              for g in range(SWA_GROUP)], axis=0)
        s = lax.dot_general(qs, k3, (((1,), (1,)), ((), ())), preferred_element_type=f32)
        s = jnp.where(valid4, s, -1e30)
        sink = jnp.zeros((SWA_GROUP * BLOCK, 1), f32)
        for g in range(SWA_GROUP):
            sink = jnp.where(grp == g, sink_ref[hk * SWA_GROUP + g], sink)
        m = jnp.maximum(jnp.max(s, axis=-1, keepdims=True), sink)
        e = jnp.exp(s - m)
        l = jnp.sum(e, axis=-1, keepdims=True) + jnp.exp(sink - m)
        p = (e * (1.0 / l)).astype(bf16)
        o = jnp.dot(p, v3, preferred_element_type=f32)
        outs.extend(o[g * BLOCK:(g + 1) * BLOCK] for g in range(SWA_GROUP))
    for pc in range(SWA_HEADS // 2):
        hk = (2 * pc) // SWA_GROUP
        left, right = outs[2 * pc], outs[2 * pc + 1]
        if hk == 1:
            left = pltpu.roll(left, HEAD_DIM, 1)
        else:
            right = pltpu.roll(right, HEAD_DIM, 1)
        o_ref[:, pc * LANES:(pc + 1) * LANES] = jnp.where(lane < HEAD_DIM, left, right).astype(bf16)


def _band_attn(sink, sqp, sk, sv, batch, seq):
    t = sqp.shape[0]
    nb = seq // BLOCK
    prev = lambda b, n: (b * nb + jnp.maximum(n - 1, 0), 0)
    cur = lambda b, n: (b * nb + n, 0)
    nxt = lambda b, n: (b * nb + jnp.minimum(n + 1, nb - 1), 0)
    kv = lambda f: pl.BlockSpec((BLOCK, LANES), f)
    return pl.pallas_call(
        _band_attn_kernel,
        grid=(batch, nb),
        in_specs=[pl.BlockSpec(memory_space=pltpu.SMEM), pl.BlockSpec((BLOCK, SWA_HEADS * LANES), cur),
                  kv(prev), kv(cur), kv(nxt), kv(prev), kv(cur), kv(nxt)],
        out_specs=pl.BlockSpec((BLOCK, SWA_WIDTH), cur),
        out_shape=jax.ShapeDtypeStruct((t, SWA_WIDTH), bf16),
        compiler_params=_cparams(("arbitrary", "arbitrary")),
        name="band_attn",
    )(sink, sqp, sk, sk, sk, sv, sv, sv)


def _attn_out_kernel(do_ref, so_ref, x_ref, w_ref, g_ref, rw_ref, rhi_ref, h_ref, hn_ref, aff_ref):
    mix = jnp.concatenate([do_ref[...], so_ref[...]], axis=1)
    h = x_ref[...] + jnp.dot(mix, w_ref[...], preferred_element_type=f32)
    h_ref[...] = h
    ms = jnp.mean(h * h, axis=-1, keepdims=True)
    hn = h * lax.rsqrt(ms + EPS) * g_ref[...]
    hn_ref[...] = hn.reshape(hn.shape[0], ROW_TILES, LANES)
    xh = hn.astype(bf16)
    xl = (hn - xh.astype(f32)).astype(bf16)
    both = jnp.dot(xh, rw_ref[...], preferred_element_type=f32)
    logits = both[:, 0:LANES] + (both[:, LANES:2 * LANES] + jnp.dot(xl, rhi_ref[...], preferred_element_type=f32))
    lane = _lane_iota(logits.shape)
    live = lane < N_EXPERTS
    lm = jnp.where(live, logits, -1e30)
    m = jnp.max(lm, axis=-1, keepdims=True)
    e = jnp.where(live, jnp.exp(lm - m), 0.0)
    aff = e / jnp.sum(e, axis=-1, keepdims=True)
    aff_ref[...] = aff[:, 0:N_EXPERTS]


def _attn_out(do, so, x2, w, g, rw, rhi, tb):
    t = x2.shape[0]
    row = lambda i: (i, 0)
    const = lambda i: (0, 0)
    return pl.pallas_call(
        _attn_out_kernel,
        grid=(t // tb,),
        in_specs=[pl.BlockSpec((tb, 512), row), pl.BlockSpec((tb, 512), row), pl.BlockSpec((tb, D_MODEL), row),
                  pl.BlockSpec((D_MODEL, D_MODEL), const),
                  pl.BlockSpec((1, D_MODEL), const), pl.BlockSpec((D_MODEL, 2 * LANES), const),
                  pl.BlockSpec((D_MODEL, LANES), const)],
        out_specs=[pl.BlockSpec((tb, D_MODEL), row), pl.BlockSpec((tb, ROW_TILES, LANES), lambda i: (i, 0, 0)),
                   pl.BlockSpec((tb, N_EXPERTS), row)],
        out_shape=[jax.ShapeDtypeStruct((t, D_MODEL), f32), jax.ShapeDtypeStruct((t, ROW_TILES, LANES), f32),
                   jax.ShapeDtypeStruct((t, N_EXPERTS), f32)],
        compiler_params=_cparams(("arbitrary",)),
        name="attn_out",
    )(do, so, x2, w, g, rw, rhi)


def _class_sum(v):
    v = v + pltpu.roll(v, 64, 1)
    v = v + pltpu.roll(v, 32, 1)
    return v + pltpu.roll(v, 16, 1)


def _select_kernel(a_ref, o_ref, *, capacity):
    bits = pltpu.bitcast(a_ref[...], jnp.int32)

    def count_ge(cand):
        cnt = jnp.sum((bits >= cand[0:1, :]).astype(f32), axis=0, keepdims=True)
        return _class_sum(jnp.broadcast_to(cnt, (8, LANES)))

    def body(i, thr):
        cand = thr | jnp.left_shift(jnp.int32(1), 30 - i)
        return jnp.where(count_ge(cand) >= capacity, cand, thr)

    thr = lax.fori_loop(0, 31, body, jnp.zeros((8, LANES), jnp.int32))
    n_gt = count_ge(thr + 1)
    row = lax.broadcasted_iota(jnp.int32, (8, LANES), 0)
    o_ref[...] = jnp.where(row == 0, pltpu.bitcast(thr, f32), capacity - n_gt)


def _select(aff_dense, capacity):
    return pl.pallas_call(
        functools.partial(_select_kernel, capacity=float(capacity)),
        out_shape=jax.ShapeDtypeStruct((8, LANES), f32),
        compiler_params=pltpu.CompilerParams(vmem_limit_bytes=VMEM_LIMIT),
        name="select",
    )(aff_dense)


def _rank_kernel(aff_ref, thr_ref, need_ref, tok_ref, rank_ref, offs_out_ref, carry_ref, offs_ref):
    c = pl.program_id(0)

    @pl.when(c == 0)
    def _():
        tok_ref[...] = jnp.zeros_like(tok_ref)
        carry_ref[...] = jnp.zeros_like(carry_ref)
        for e in range(N_EXPERTS):
            offs_ref[e] = 0

    a = aff_ref[...]
    thr = thr_ref[...]
    gt = a > thr
    eq = a == thr
    ri = lax.broadcasted_iota(jnp.int32, (RANK_CHUNK, RANK_CHUNK), 0)
    ci = lax.broadcasted_iota(jnp.int32, (RANK_CHUNK, RANK_CHUNK), 1)
    lower = (ci < ri).astype(bf16)
    eqf = eq.astype(bf16)
    eq_rank = jnp.dot(lower, eqf, preferred_element_type=f32) + carry_ref[0:1, :]
    sel = gt | (eq & (eq_rank < need_ref[...]))
    self_ = sel.astype(bf16)
    rank = jnp.dot(lower, self_, preferred_element_type=f32) + carry_ref[1:2, :]
    rank_ref[...] = jnp.where(sel, rank, -1.0)
    n_sel = jnp.sum(sel.astype(f32), axis=0, keepdims=True)
    carry_ref[0:1, :] = carry_ref[0:1, :] + jnp.sum(eq.astype(f32), axis=0, keepdims=True)
    carry_ref[1:2, :] = carry_ref[1:2, :] + n_sel

    sub = lax.broadcasted_iota(jnp.int32, (8, RANK_CHUNK), 0)
    loc = lax.broadcasted_iota(jnp.int32, (8, RANK_CHUNK), 1)
    lhs = jnp.where(sub == 0, loc, jnp.where(sub == 1, 1, 0)).astype(f32).astype(bf16)
    slot = lax.broadcasted_iota(jnp.int32, (RANK_CHUNK, 3 * LANES), 1).astype(f32)
    base = (c * RANK_CHUNK).astype(f32)
    for e in range(N_EXPERTS):
        off = offs_ref[e]
        j0 = off // LANES
        rel = rank[:, e:e + 1] - (j0 * LANES).astype(f32)
        onehot = jnp.where((rel == slot) & sel[:, e:e + 1], 1.0, 0.0).astype(bf16)
        placed = jnp.dot(lhs, onehot, preferred_element_type=f32)
        row = lax.broadcasted_iota(jnp.int32, placed.shape, 0)
        filled = jnp.broadcast_to(placed[1:2, :], placed.shape)
        placed = jnp.where(row == 0, placed + base * filled, placed)
        for k in range(3):
            tok_ref[e, j0 + k] = tok_ref[e, j0 + k] + placed[:, k * LANES:(k + 1) * LANES]
        new_off = off + jnp.sum(n_sel[:, e:e + 1]).astype(jnp.int32)
        offs_ref[e] = new_off
        offs_out_ref[c * N_EXPERTS + e] = off
        offs_out_ref[(c + 1) * N_EXPERTS + e] = new_off


def _rank(aff16, thr, need, capacity):
    t = aff16.shape[0]
    n_chunks = t // RANK_CHUNK
    n_tiles = capacity // LANES + 3
    const2 = lambda c: (0, 0)
    return pl.pallas_call(
        _rank_kernel,
        grid=(n_chunks,),
        in_specs=[pl.BlockSpec((RANK_CHUNK, N_EXPERTS), lambda c: (c, 0)),
                  pl.BlockSpec((1, N_EXPERTS), const2), pl.BlockSpec((1, N_EXPERTS), const2)],
        out_specs=[pl.BlockSpec((N_EXPERTS, n_tiles, 8, LANES), lambda c: (0, 0, 0, 0)),
                   pl.BlockSpec((RANK_CHUNK, N_EXPERTS), lambda c: (c, 0)),
                   pl.BlockSpec(memory_space=pltpu.SMEM)],
        out_shape=[jax.ShapeDtypeStruct((N_EXPERTS, n_tiles, 8, LANES), f32),
                   jax.ShapeDtypeStruct((t, N_EXPERTS), f32),
                   jax.ShapeDtypeStruct(((n_chunks + 1) * N_EXPERTS,), jnp.int32)],
        scratch_shapes=[pltpu.VMEM((8, N_EXPERTS), f32), pltpu.SMEM((N_EXPERTS,), jnp.int32)],
        compiler_params=_cparams(("arbitrary",)),
        name="rank",
    )(aff16, thr, need)


DMA_UNROLL = 8
FFN_COL_BLOCKS = 4


def _ffn_kernel(tok_ref, nxt_ref, rows_ref, wg_ref, wu_ref, wd_ref, o_ref, xbuf, sem):
    s = pl.program_id(0)
    n = pl.num_programs(0) - 1
    slot = s % 2
    tc = xbuf.shape[1]

    def row_copy(idx_ref, k, r):
        return pltpu.make_async_copy(rows_ref.at[pl.ds(idx_ref[0, 0, r], 1)], xbuf.at[k, pl.ds(r, 1)], sem.at[k])

    def wait_rows(k):
        pltpu.make_async_copy(rows_ref.at[pl.ds(0, tc)], xbuf.at[k], sem.at[k]).wait()

    @pl.when(s == 0)
    def _():
        def body(r, carry):
            row_copy(tok_ref, slot, r).start()
            return carry
        lax.fori_loop(0, tc, body, 0, unroll=DMA_UNROLL)

    @pl.when(s < n)
    def _():
        wait_rows(slot)
        x = xbuf[slot].reshape(tc, D_MODEL).astype(bf16)
        per = tc // (2 * FFN_COL_BLOCKS)

        def issue(group):
            for r in range(group * per, (group + 1) * per):
                row_copy(nxt_ref, 1 - slot, r).start()

        cw = EXPERT_FF // FFN_COL_BLOCKS
        mids = []
        for b in range(FFN_COL_BLOCKS):
            issue(b)
            g = jnp.dot(x, wg_ref[0, :, b * cw:(b + 1) * cw], preferred_element_type=f32)
            u = jnp.dot(x, wu_ref[0, :, b * cw:(b + 1) * cw], preferred_element_type=f32)
            mids.append((g * jax.nn.sigmoid(g) * u).astype(bf16))
        hmid = jnp.concatenate(mids, axis=1)
        ow = D_MODEL // FFN_COL_BLOCKS
        for b in range(FFN_COL_BLOCKS):
            issue(FFN_COL_BLOCKS + b)
            o_ref[:, b * ow:(b + 1) * ow] = jnp.dot(hmid, wd_ref[0, :, b * ow:(b + 1) * ow],
                                                    preferred_element_type=f32).astype(bf16)

    @pl.when(s == n)
    def _():
        wait_rows(slot)
        o_ref[...] = jnp.zeros_like(o_ref)


def _ffn(tok, rows, wg, wu, wd, capacity, tc):
    nt = capacity // tc
    n = N_EXPERTS * nt
    expert = lambda s: (jnp.minimum(s // nt, N_EXPERTS - 1), 0, 0)
    return pl.pallas_call(
        _ffn_kernel,
        grid=(n + 1,),
        in_specs=[pl.BlockSpec((1, 1, tc), lambda s: (jnp.minimum(s, n - 1), 0, 0), memory_space=pltpu.SMEM),
                  pl.BlockSpec((1, 1, tc), lambda s: (jnp.minimum(s + 1, n - 1), 0, 0), memory_space=pltpu.SMEM),
                  pl.BlockSpec(memory_space=pl.ANY),
                  pl.BlockSpec((1, D_MODEL, EXPERT_FF), expert), pl.BlockSpec((1, D_MODEL, EXPERT_FF), expert),
                  pl.BlockSpec((1, EXPERT_FF, D_MODEL), expert)],
        out_specs=pl.BlockSpec((tc, D_MODEL), lambda s: (s, 0)),
        out_shape=jax.ShapeDtypeStruct(((n + 1) * tc, D_MODEL), bf16),
        scratch_shapes=[pltpu.VMEM((2, tc, ROW_TILES, LANES), f32), pltpu.SemaphoreType.DMA((2,))],
        compiler_params=_cparams(("arbitrary",)),
        name="ffn",
    )(tok, tok, rows, wg, wu, wd)


WIN = 128
ROW_ALIGN = 16


def _combine_kernel(offs_ref, h_ref, rank_ref, aff_ref, ye_ref, y_ref, win, sem, *, capacity):
    c = pl.program_id(0)
    n_chunks = pl.num_programs(0)
    slot = c % 2
    width = N_EXPERTS * WIN

    def window_starts(chunk):
        return [(offs_ref[chunk * N_EXPERTS + e] // ROW_ALIGN) * ROW_ALIGN for e in range(N_EXPERTS)]

    def copy(e, start, p, k):
        src = jnp.minimum(e * capacity + start + p * WIN, ye_ref.shape[0] - WIN)
        return pltpu.make_async_copy(ye_ref.at[pl.ds(pl.multiple_of(src, ROW_ALIGN), WIN)],
                                     win.at[k, pl.ds(e * WIN, WIN)], sem.at[k])

    starts = window_starts(c)
    n_pass = jnp.int32(0)
    for e in range(N_EXPERTS):
        st = offs_ref[c * N_EXPERTS + e]
        cnt = offs_ref[(c + 1) * N_EXPERTS + e] - st
        n_pass = jnp.maximum(n_pass, jnp.where(cnt > 0, (st - starts[e] + cnt + WIN - 1) // WIN, 0))

    @pl.when(c == 0)
    def _():
        for e in range(N_EXPERTS):
            copy(e, starts[e], 0, slot).start()

    @pl.when(c + 1 < n_chunks)
    def _():
        nxt = window_starts(c + 1)
        for e in range(N_EXPERTS):
            copy(e, nxt[e], 0, 1 - slot).start()

    col = lax.broadcasted_iota(jnp.int32, (N_EXPERTS, width), 1)
    row = lax.broadcasted_iota(jnp.int32, (N_EXPERTS, width), 0)
    spread = jnp.where((col // WIN) == row, 1.0, 0.0).astype(bf16)
    rank = rank_ref[...]
    hi = jnp.floor(rank * (1.0 / LANES))
    lo = rank - hi * LANES
    rank_rep = LANES * jnp.dot(hi.astype(bf16), spread, preferred_element_type=f32) \
        + jnp.dot(lo.astype(bf16), spread, preferred_element_type=f32)
    gate_rep = jnp.dot(aff_ref[...].astype(bf16), spread, preferred_element_type=f32)
    col1 = lax.broadcasted_iota(jnp.int32, (1, width), 1)
    base = jnp.zeros((1, width), jnp.int32)
    for e in range(N_EXPERTS):
        base = jnp.where((col1 // WIN) == e, starts[e], base)
    base = (base + col1 % WIN).astype(f32)

    def scatter_matmul(p, k):
        hit = rank_rep == base + (p * WIN).astype(f32)
        sel = jnp.where(hit, gate_rep, 0.0).astype(bf16)
        return jnp.dot(sel, win[k], preferred_element_type=f32)

    def wait_windows(k):
        for e in range(N_EXPERTS):
            copy(e, starts[e], 0, k).wait()

    wait_windows(slot)
    acc = scatter_matmul(jnp.int32(0), slot)

    def later_pass(p, acc):
        for e in range(N_EXPERTS):
            copy(e, starts[e], p, slot).start()
        wait_windows(slot)
        return acc + scatter_matmul(p, slot)

    y_ref[...] = h_ref[...] + lax.fori_loop(1, n_pass, later_pass, acc)


def _combine(offs, h, rank16, aff16, ye, capacity):
    t = h.shape[0]
    row = lambda c, offs: (c, 0)
    return pl.pallas_call(
        functools.partial(_combine_kernel, capacity=capacity),
        grid_spec=pltpu.PrefetchScalarGridSpec(
            num_scalar_prefetch=1,
            grid=(t // RANK_CHUNK,),
            in_specs=[pl.BlockSpec((RANK_CHUNK, D_MODEL), row), pl.BlockSpec((RANK_CHUNK, N_EXPERTS), row),
                      pl.BlockSpec((RANK_CHUNK, N_EXPERTS), row), pl.BlockSpec(memory_space=pl.ANY)],
            out_specs=pl.BlockSpec((RANK_CHUNK, D_MODEL), row),
            scratch_shapes=[pltpu.VMEM((2, N_EXPERTS * WIN, D_MODEL), bf16), pltpu.SemaphoreType.DMA((2,))]),
        out_shape=jax.ShapeDtypeStruct((t, D_MODEL), f32),
        compiler_params=_cparams(("arbitrary",)),
        name="combine",
    )(offs, h, rank16, aff16, ye)


def _rope_tables(seq):
    half = HEAD_DIM // 2
    inv = 1.0 / (ROPE_THETA ** (jnp.arange(0, HEAD_DIM, 2, dtype=f32) / HEAD_DIM))
    ang = jnp.arange(seq, dtype=f32)[:, None] * inv[None, :]
    cos = jnp.tile(jnp.cos(ang), (1, LANES // half))
    sign = jnp.where((jnp.arange(LANES) % HEAD_DIM) < half, -1.0, 1.0).astype(f32)
    sin = jnp.tile(jnp.sin(ang), (1, LANES // half)) * sign[None, :]
    return cos, sin


def _block_diag_mean(width):
    i = jnp.arange(width)
    return jnp.where((i[:, None] // HEAD_DIM) == (i[None, :] // HEAD_DIM), 1.0 / HEAD_DIM, 0.0).astype(bf16)


def _pick_tile(n, pref):
    while n % pref:
        pref //= 2
    return pref


def _trunk(x, p):
    batch, seq, _ = x.shape
    t = batch * seq
    capacity = CAPACITY_FACTOR * t // N_EXPERTS
    assert seq % BLOCK == 0 and t % RANK_CHUNK == 0 and capacity % LANES == 0
    tb = _pick_tile(seq, 512)
    tq = _pick_tile(seq, 256)
    kc = _pick_tile(seq, DIFF_KEY_CHUNK)
    heads = DIFF_HEADS if seq <= 2048 else DIFF_HEADS // 2
    tc = _pick_tile(capacity, FFN_TILE)
    cos, sin = _rope_tables(seq)
    gm512, gm128 = _block_diag_mean(512), _block_diag_mean(128)
    x2 = x.reshape(t, D_MODEL)
    depth = p["w_in"].shape[0]
    for l in range(depth):
        lam_init = 0.8 - 0.6 * math.exp(-0.3 * l)
        lam = (jnp.exp(jnp.sum(p["lambda_q1"][l] * p["lambda_k1"][l]))
               - jnp.exp(jnp.sum(p["lambda_q2"][l] * p["lambda_k2"][l])) + lam_init)
        bound = (8.0 * LOG2E * (1.0 + 2.0 ** -6)) * jnp.max(jnp.abs(p["diff_q_norm"][l])) \
            * jnp.max(jnp.abs(p["diff_k_norm"][l]))
        scal = jnp.stack([lam, jnp.asarray(1.0 - lam_init, f32), bound]).astype(f32)
        tile8 = lambda v: jnp.tile(v, 8)[None, :]
        dq, dk, dv, sqp, sk, sv = _attn_in(
            x2, p["attn_norm"][l][None, :], p["w_in_bf"][l], cos, sin,
            tile8(p["diff_q_norm"][l]), tile8(p["diff_k_norm"][l]), tile8(p["swa_q_norm"][l]),
            jnp.tile(p["swa_k_norm"][l], 2)[None, :], gm512, gm128, seq, tb)
        do = _diff_attn(scal, dq, dk, dv, p["diff_subln"][l][None, :], batch, seq, tq, heads, kc)
        so = _band_attn(p["swa_sink"][l], sqp, sk, sv, batch, seq)
        h, rows, aff16 = _attn_out(do, so, x2, p["w_out_bf"][l], p["ffn_norm"][l][None, :],
                                   jnp.concatenate([p["r_hi"][l], p["r_lo"][l]], axis=1), p["r_hi"][l], tb)
        sel = _select(aff16.reshape(t // 8, LANES), capacity)
        tokbuf, rank16, offs = _rank(aff16, sel[0:1, :N_EXPERTS], sel[1:2, :N_EXPERTS], capacity)
        tok = tokbuf[:, :capacity // LANES, 0, :].reshape(N_EXPERTS * capacity // tc, 1, tc).astype(jnp.int32)
        ye = _ffn(tok, rows, p["w_gate_bf"][l], p["w_up_bf"][l], p["w_down_bf"][l], capacity, tc)
        x2 = _combine(offs, h, rank16, aff16, ye, capacity)
    return x2.reshape(batch, seq, D_MODEL)


def kernel(x_prompt, x_sample, attn_norm, w_in, diff_q_norm, diff_k_norm, lambda_q1, lambda_k1, lambda_q2, lambda_k2,
           diff_subln, swa_q_norm, swa_k_norm, swa_sink, w_out, ffn_norm, w_router, w_gate, w_up, w_down):
    r_pad = jnp.pad(w_router, ((0, 0), (0, 0), (0, LANES - N_EXPERTS)))
    r_hi = r_pad.astype(bf16)
    p = dict(attn_norm=attn_norm, diff_q_norm=diff_q_norm, diff_k_norm=diff_k_norm, lambda_q1=lambda_q1,
             lambda_k1=lambda_k1, lambda_q2=lambda_q2, lambda_k2=lambda_k2, diff_subln=diff_subln,
             swa_q_norm=swa_q_norm, swa_k_norm=swa_k_norm, swa_sink=swa_sink, ffn_norm=ffn_norm,
             w_in=w_in, w_in_bf=w_in.astype(bf16), w_out_bf=w_out.astype(bf16),
             r_hi=r_hi, r_lo=(r_pad - r_hi.astype(f32)).astype(bf16),
             w_gate_bf=w_gate.astype(bf16), w_up_bf=w_up.astype(bf16), w_down_bf=w_down.astype(bf16))
    return (_trunk(x_prompt, p), _trunk(x_sample, p))
```

```python
import functools
import math

import jax
import jax.numpy as jnp
from jax import lax
from jax.experimental import pallas as pl
from jax.experimental.pallas import tpu as pltpu

D_MODEL = 1024
HEAD_DIM = 64
DIFF_HEADS = 4
DIFF_VDIM = 128
DIFF_WIDTH = 512
SWA_HEADS = 8
SWA_KV_HEADS = 2
SWA_GROUP = 4
SWA_WIDTH = 512
BLOCK = 128
N_EXPERTS = 16
EXPERT_FF = 1024
CAPACITY_FACTOR = 2
ROPE_THETA = 10000.0
EPS = 1e-6
LOG2E = 1.4426950408889634
EXP_SHIFT_LIMIT = 60.0
DIFF_KEY_CHUNK = 512
IN_WIDTH = 2304
LANES = 128
ROW_TILES = D_MODEL // LANES
RANK_CHUNK = 256
NOT_MEMBER = -256.0
FFN_TILE = 512
VMEM_LIMIT = 56 * 1024 * 1024

f32 = jnp.float32
bf16 = jnp.bfloat16


def _cparams(sem):
    return pltpu.CompilerParams(dimension_semantics=sem, vmem_limit_bytes=VMEM_LIMIT)


def _lane_iota(shape):
    return lax.broadcasted_iota(jnp.int32, shape, len(shape) - 1)


def _head_norm_rope(u, gain, gmat, cos, sin_signed):
    w = u.shape[1]
    ms = jnp.dot((u * u).astype(bf16), gmat, preferred_element_type=f32)
    un = u * lax.rsqrt(ms + EPS) * gain
    lane = _lane_iota(un.shape)
    first_half = (lane % HEAD_DIM) < (HEAD_DIM // 2)
    rot = jnp.where(first_half, pltpu.roll(un, w - HEAD_DIM // 2, 1), pltpu.roll(un, HEAD_DIM // 2, 1))
    return un * cos + rot * sin_signed


def _attn_in_kernel(x_ref, g_ref, w_ref, cos_ref, sin_ref, gdq_ref, gdk_ref, gsq_ref, gsk_ref, gm512_ref, gm128_ref,
                    dq_ref, dk_ref, dv_ref, sq_ref, sk_ref, sv_ref):
    x = x_ref[...]
    ms = jnp.mean(x * x, axis=-1, keepdims=True)
    xn = (x * lax.rsqrt(ms + EPS) * g_ref[...]).astype(bf16)
    z = jnp.dot(xn, w_ref[...], preferred_element_type=f32)
    cos1, sin1 = cos_ref[...], sin_ref[...]
    cos4 = jnp.concatenate([cos1] * 4, axis=1)
    sin4 = jnp.concatenate([sin1] * 4, axis=1)
    gm512 = gm512_ref[...]
    scale = 1.0 / math.sqrt(HEAD_DIM)
    dq = _head_norm_rope(z[:, 0:512], gdq_ref[...], gm512, cos4, sin4)
    dk = _head_norm_rope(z[:, 512:1024], gdk_ref[...], gm512, cos4, sin4)
    sq = _head_norm_rope(z[:, 1536:2048], gsq_ref[...], gm512, cos4, sin4)
    sk = _head_norm_rope(z[:, 2048:2176], gsk_ref[...], gm128_ref[...], cos1, sin1)
    dq_ref[...] = (dq * (scale * LOG2E)).astype(bf16)
    dk_ref[...] = dk.astype(bf16)
    dv_ref[...] = z[:, 1024:1536].astype(bf16)
    sk_ref[...] = sk.astype(bf16)
    sv_ref[...] = z[:, 2176:2304].astype(bf16)
    sq = sq * scale
    lane = _lane_iota((sq.shape[0], LANES))
    for j in range(SWA_HEADS):
        blk = sq[:, (j // 2) * LANES:(j // 2 + 1) * LANES]
        hk = j // SWA_GROUP
        if (j % 2) != hk:
            blk = pltpu.roll(blk, HEAD_DIM, 1)
        blk = jnp.where((lane // HEAD_DIM) == hk, blk, 0.0)
        sq_ref[:, j * LANES:(j + 1) * LANES] = blk.astype(bf16)


def _attn_in(x2, g, w_in, cos, sin, gdq, gdk, gsq, gsk, gm512, gm128, seq, tb):
    t = x2.shape[0]
    nb_seq = seq // tb
    row = lambda i: (i, 0)
    const = lambda i: (0, 0)
    outs = [jax.ShapeDtypeStruct((t, 512), bf16)] * 3 + [jax.ShapeDtypeStruct((t, 1024), bf16)] + \
           [jax.ShapeDtypeStruct((t, 128), bf16)] * 2
    return pl.pallas_call(
        _attn_in_kernel,
        grid=(t // tb,),
        in_specs=[
            pl.BlockSpec((tb, D_MODEL), row),
            pl.BlockSpec((1, D_MODEL), const),
            pl.BlockSpec((D_MODEL, IN_WIDTH), const),
            pl.BlockSpec((tb, LANES), lambda i: (i % nb_seq, 0)),
            pl.BlockSpec((tb, LANES), lambda i: (i % nb_seq, 0)),
            pl.BlockSpec((1, 512), const), pl.BlockSpec((1, 512), const), pl.BlockSpec((1, 512), const),
            pl.BlockSpec((1, 128), const),
            pl.BlockSpec((512, 512), const), pl.BlockSpec((128, 128), const),
        ],
        out_specs=[pl.BlockSpec((tb, 512), row)] * 3 + [pl.BlockSpec((tb, 1024), row)] + [pl.BlockSpec((tb, 128), row)] * 2,
        out_shape=outs,
        compiler_params=_cparams(("arbitrary",)),
        name="attn_in",
    )(x2, g, w_in, cos, sin, gdq, gdk, gsq, gsk, gm512, gm128)


def _diff_attn_kernel(scal_ref, q_ref, k_ref, v_ref, g_ref, o_ref, *, kc):
    lam = scal_ref[0]
    out_scale = scal_ref[1]
    bound = scal_ref[2]
    tq = q_ref.shape[0]
    n_heads = q_ref.shape[1] // LANES
    n_chunks = k_ref.shape[0] // kc
    lane = _lane_iota((tq, LANES))

    def stacked_q(h):
        q = q_ref[:, h * LANES:(h + 1) * LANES]
        zero = jnp.zeros_like(q)
        return jnp.concatenate([jnp.where(lane < HEAD_DIM, q, zero), jnp.where(lane >= HEAD_DIM, q, zero)], axis=0)

    def value_block(rows, h):
        vblk = v_ref[rows, h * LANES:(h + 1) * LANES]
        return jnp.concatenate([vblk, jnp.ones_like(vblk)], axis=1)

    def finish(ev, h):
        on = ev[:, 0:LANES] * (1.0 / ev[:, LANES:LANES + 1])
        o = on[:tq] - lam * on[tq:]
        ms = jnp.mean(o * o, axis=-1, keepdims=True)
        o_ref[:, h * LANES:(h + 1) * LANES] = ((o * lax.rsqrt(ms + EPS) * g_ref[...]) * out_scale).astype(bf16)

    @pl.when(bound <= EXP_SHIFT_LIMIT)
    def _():
        for h in range(n_heads):
            qs = stacked_q(h)
            ev = jnp.zeros((2 * tq, 2 * LANES), f32)
            for c in range(n_chunks):
                rows = slice(c * kc, (c + 1) * kc)
                s = lax.dot_general(qs, k_ref[rows, h * LANES:(h + 1) * LANES], (((1,), (1,)), ((), ())),
                                    preferred_element_type=f32)
                ev = ev + jnp.dot(jnp.exp2(s - bound).astype(bf16), value_block(rows, h), preferred_element_type=f32)
            finish(ev, h)

    @pl.when(bound > EXP_SHIFT_LIMIT)
    def _():
        for h in range(n_heads):
            qs = stacked_q(h)

            def chunk(c, carry, qs=qs, h=h):
                m, ev = carry
                rows = pl.ds(pl.multiple_of(c * kc, kc), kc)
                s = lax.dot_general(qs, k_ref[rows, h * LANES:(h + 1) * LANES], (((1,), (1,)), ((), ())),
                                    preferred_element_type=f32)
                m_new = jnp.maximum(m, jnp.max(s, axis=-1, keepdims=True))
                ev = jnp.exp2(m - m_new) * ev + jnp.dot(jnp.exp2(s - m_new).astype(bf16), value_block(rows, h),
                                                        preferred_element_type=f32)
                return m_new, ev

            init = (jnp.full((2 * tq, 1), -jnp.inf, f32), jnp.zeros((2 * tq, 2 * LANES), f32))
            finish(lax.fori_loop(0, n_chunks, chunk, init)[1], h)


def _diff_attn(scal, dq, dk, dv, g, batch, seq, tq, heads, kc):
    t = dq.shape[0]
    nq = seq // tq
    w = heads * LANES
    return pl.pallas_call(
        functools.partial(_diff_attn_kernel, kc=kc),
        grid=(batch, DIFF_HEADS // heads, nq),
        in_specs=[
            pl.BlockSpec(memory_space=pltpu.SMEM),
            pl.BlockSpec((tq, w), lambda b, h, i: (b * nq + i, h)),
            pl.BlockSpec((seq, w), lambda b, h, i: (b, h)),
            pl.BlockSpec((seq, w), lambda b, h, i: (b, h)),
            pl.BlockSpec((1, LANES), lambda b, h, i: (0, 0)),
        ],
        out_specs=pl.BlockSpec((tq, w), lambda b, h, i: (b * nq + i, h)),
        out_shape=jax.ShapeDtypeStruct((t, DIFF_WIDTH), bf16),
        compiler_params=_cparams(("arbitrary", "arbitrary", "arbitrary")),
        name="diff_attn",
    )(scal, dq, dk, dv, g)


def _band_attn_kernel(sink_ref, q_ref, kp_ref, kc_ref, kn_ref, vp_ref, vc_ref, vn_ref, o_ref):
    n = pl.program_id(1)
    nb = pl.num_programs(1)
    k3 = jnp.concatenate([kp_ref[...], kc_ref[...], kn_ref[...]], axis=0)
    v3 = jnp.concatenate([vp_ref[...], vc_ref[...], vn_ref[...]], axis=0)
    r = lax.broadcasted_iota(jnp.int32, (BLOCK, 3 * BLOCK), 0)
    c = lax.broadcasted_iota(jnp.int32, (BLOCK, 3 * BLOCK), 1)
    valid = (c >= r) & (c <= r + 2 * BLOCK)
    valid = valid & ((c >= BLOCK) | (n > 0)) & ((c < 2 * BLOCK) | (n < nb - 1))
    lane = _lane_iota((BLOCK, LANES))
    valid4 = jnp.concatenate([valid] * SWA_GROUP, axis=0)
    grp = lax.broadcasted_iota(jnp.int32, (SWA_GROUP * BLOCK, 1), 0) // BLOCK
    outs = []
    for hk in range(SWA_KV_HEADS):
        qs = jnp.concatenate([q_ref[:, (hk * SWA_GROUP + g) * LANES:(hk * SWA_GROUP + g + 1) * LANES]
                              for g in range(SWA_GROUP)], axis=0)
        s = lax.dot_general(qs, k3, (((1,), (1,)), ((), ())), preferred_element_type=f32)
        s = jnp.where(valid4, s, -1e30)
        sink = jnp.zeros((SWA_GROUP * BLOCK, 1), f32)
        for g in range(SWA_GROUP):
            sink = jnp.where(grp == g, sink_ref[hk * SWA_GROUP + g], sink)
        m = jnp.maximum(jnp.max(s, axis=-1, keepdims=True), sink)
        e = jnp.exp(s - m)
        l = jnp.sum(e, axis=-1, keepdims=True) + jnp.exp(sink - m)
        p = (e * (1.0 / l)).astype(bf16)
        o = jnp.dot(p, v3, preferred_element_type=f32)
        outs.extend(o[g * BLOCK:(g + 1) * BLOCK] for g in range(SWA_GROUP))
    for pc in range(SWA_HEADS // 2):
        hk = (2 * pc) // SWA_GROUP
        left, right = outs[2 * pc], outs[2 * pc + 1]
        if hk == 1:
            left = pltpu.roll(left, HEAD_DIM, 1)
        else:
            right = pltpu.roll(right, HEAD_DIM, 1)
        o_ref[:, pc * LANES:(pc + 1) * LANES] = jnp.where(lane < HEAD_DIM, left, right).astype(bf16)


def _band_attn(sink, sqp, sk, sv, batch, seq):
    t = sqp.shape[0]
    nb = seq // BLOCK
    prev = lambda b, n: (b * nb + jnp.maximum(n - 1, 0), 0)
    cur = lambda b, n: (b * nb + n, 0)
    nxt = lambda b, n: (b * nb + jnp.minimum(n + 1, nb - 1), 0)
    kv = lambda f: pl.BlockSpec((BLOCK, LANES), f)
    return pl.pallas_call(
        _band_attn_kernel,
        grid=(batch, nb),
        in_specs=[pl.BlockSpec(memory_space=pltpu.SMEM), pl.BlockSpec((BLOCK, SWA_HEADS * LANES), cur),
                  kv(prev), kv(cur), kv(nxt), kv(prev), kv(cur), kv(nxt)],
        out_specs=pl.BlockSpec((BLOCK, SWA_WIDTH), cur),
        out_shape=jax.ShapeDtypeStruct((t, SWA_WIDTH), bf16),
        compiler_params=_cparams(("arbitrary", "arbitrary")),
        name="band_attn",
    )(sink, sqp, sk, sk, sk, sv, sv, sv)


def _attn_out_kernel(do_ref, so_ref, x_ref, w_ref, g_ref, rw_ref, rhi_ref, h_ref, hn_ref, aff_ref, afft_ref):
    mix = jnp.concatenate([do_ref[...], so_ref[...]], axis=1)
    h = x_ref[...] + jnp.dot(mix, w_ref[...], preferred_element_type=f32)
    h_ref[...] = h
    ms = jnp.mean(h * h, axis=-1, keepdims=True)
    hn = h * lax.rsqrt(ms + EPS) * g_ref[...]
    hn_ref[...] = hn.reshape(hn.shape[0], ROW_TILES, LANES)
    xh = hn.astype(bf16)
    xl = (hn - xh.astype(f32)).astype(bf16)
    both = jnp.dot(xh, rw_ref[...], preferred_element_type=f32)
    logits = both[:, 0:LANES] + (both[:, LANES:2 * LANES] + jnp.dot(xl, rhi_ref[...], preferred_element_type=f32))
    lane = _lane_iota(logits.shape)
    live = lane < N_EXPERTS
    lm = jnp.where(live, logits, -1e30)
    m = jnp.max(lm, axis=-1, keepdims=True)
    e = jnp.where(live, jnp.exp(lm - m), 0.0)
    aff = e / jnp.sum(e, axis=-1, keepdims=True)
    aff_ref[...] = aff[:, 0:N_EXPERTS]
    afft_ref[...] = aff.T[0:N_EXPERTS, :]


def _attn_out(do, so, x2, w, g, rw, rhi, tb):
    t = x2.shape[0]
    row = lambda i: (i, 0)
    const = lambda i: (0, 0)
    return pl.pallas_call(
        _attn_out_kernel,
        grid=(t // tb,),
        in_specs=[pl.BlockSpec((tb, 512), row), pl.BlockSpec((tb, 512), row), pl.BlockSpec((tb, D_MODEL), row),
                  pl.BlockSpec((D_MODEL, D_MODEL), const),
                  pl.BlockSpec((1, D_MODEL), const), pl.BlockSpec((D_MODEL, 2 * LANES), const),
                  pl.BlockSpec((D_MODEL, LANES), const)],
        out_specs=[pl.BlockSpec((tb, D_MODEL), row), pl.BlockSpec((tb, ROW_TILES, LANES), lambda i: (i, 0, 0)),
                   pl.BlockSpec((tb, N_EXPERTS), row), pl.BlockSpec((N_EXPERTS, tb), lambda i: (0, i))],
        out_shape=[jax.ShapeDtypeStruct((t, D_MODEL), f32), jax.ShapeDtypeStruct((t, ROW_TILES, LANES), f32),
                   jax.ShapeDtypeStruct((t, N_EXPERTS), f32), jax.ShapeDtypeStruct((N_EXPERTS, t), f32)],
        compiler_params=_cparams(("arbitrary",)),
        name="attn_out",
    )(do, so, x2, w, g, rw, rhi)


def _class_sum(v):
    v = v + pltpu.roll(v, 64, 1)
    v = v + pltpu.roll(v, 32, 1)
    return v + pltpu.roll(v, 16, 1)


def _select_kernel(a_ref, o_ref, *, capacity):
    bits = pltpu.bitcast(a_ref[...], jnp.int32)

    def count_ge(cand):
        cnt = jnp.sum((bits >= cand[0:1, :]).astype(f32), axis=0, keepdims=True)
        return _class_sum(jnp.broadcast_to(cnt, (8, LANES)))

    def body(i, thr):
        cand = thr | jnp.left_shift(jnp.int32(1), 30 - i)
        return jnp.where(count_ge(cand) >= capacity, cand, thr)

    thr = lax.fori_loop(0, 31, body, jnp.zeros((8, LANES), jnp.int32))
    n_gt = count_ge(thr + 1)
    row = lax.broadcasted_iota(jnp.int32, (8, LANES), 0)
    o_ref[...] = jnp.where(row == 0, pltpu.bitcast(thr, f32), capacity - n_gt)


def _select(aff_dense, capacity):
    return pl.pallas_call(
        functools.partial(_select_kernel, capacity=float(capacity)),
        out_shape=jax.ShapeDtypeStruct((8, LANES), f32),
        compiler_params=pltpu.CompilerParams(vmem_limit_bytes=VMEM_LIMIT),
        name="select",
    )(aff_dense)


def _rank_kernel(afft_ref, thr_ref, need_ref, tok_ref, rank_ref, offs_out_ref, carry_ref, offs_ref):
    c = pl.program_id(0)

    @pl.when(c == 0)
    def _():
        tok_ref[...] = jnp.zeros_like(tok_ref)
        carry_ref[...] = jnp.zeros_like(carry_ref)
        for e in range(N_EXPERTS):
            offs_ref[e] = 0

    a = afft_ref[...]
    thr = thr_ref[...]
    gt = a > thr
    eq = a == thr
    ri = lax.broadcasted_iota(jnp.int32, (RANK_CHUNK, RANK_CHUNK), 0)
    ci = lax.broadcasted_iota(jnp.int32, (RANK_CHUNK, RANK_CHUNK), 1)
    before = (ri < ci).astype(bf16)
    eq_rank = jnp.dot(eq.astype(bf16), before, preferred_element_type=f32) + carry_ref[:, 0:1]
    sel = gt | (eq & (eq_rank < need_ref[...]))
    local = jnp.dot(sel.astype(bf16), before, preferred_element_type=f32)
    local = jnp.where(sel, local, NOT_MEMBER)
    rank_ref[...] = local.T
    n_sel = jnp.sum(sel.astype(f32), axis=1, keepdims=True)
    carry_ref[:, 0:1] = carry_ref[:, 0:1] + jnp.sum(eq.astype(f32), axis=1, keepdims=True)

    sub = lax.broadcasted_iota(jnp.int32, (8, RANK_CHUNK), 0)
    loc = lax.broadcasted_iota(jnp.int32, (8, RANK_CHUNK), 1)
    lhs = jnp.where(sub == 0, loc, jnp.where(sub == 1, 1, 0)).astype(f32).astype(bf16)
    slot = lax.broadcasted_iota(jnp.int32, (3 * LANES, RANK_CHUNK), 0).astype(f32)
    base = (c * RANK_CHUNK).astype(f32)
    for e in range(N_EXPERTS):
        off = offs_ref[e]
        j0 = off // LANES
        shift = (off - j0 * LANES).astype(f32)
        row_rank = local[e:e + 1, :]
        rel = jnp.where(row_rank >= 0.0, row_rank + shift, -1.0)
        onehot = jnp.where(slot == rel, 1.0, 0.0).astype(bf16)
        placed = lax.dot_general(lhs, onehot, (((1,), (1,)), ((), ())), preferred_element_type=f32)
        row = lax.broadcasted_iota(jnp.int32, placed.shape, 0)
        filled = jnp.broadcast_to(placed[1:2, :], placed.shape)
        placed = jnp.where(row == 0, placed + base * filled, placed)
        for k in range(3):
            tok_ref[e, j0 + k] = tok_ref[e, j0 + k] + placed[:, k * LANES:(k + 1) * LANES]
        new_off = off + jnp.sum(n_sel[e:e + 1, :]).astype(jnp.int32)
        offs_ref[e] = new_off
        offs_out_ref[c * N_EXPERTS + e] = off
        offs_out_ref[(c + 1) * N_EXPERTS + e] = new_off


def _rank(afft, thr, need, capacity):
    t = afft.shape[1]
    n_chunks = t // RANK_CHUNK
    n_tiles = capacity // LANES + 3
    const2 = lambda c: (0, 0)
    return pl.pallas_call(
        _rank_kernel,
        grid=(n_chunks,),
        in_specs=[pl.BlockSpec((N_EXPERTS, RANK_CHUNK), lambda c: (0, c)),
                  pl.BlockSpec((N_EXPERTS, 1), const2), pl.BlockSpec((N_EXPERTS, 1), const2)],
        out_specs=[pl.BlockSpec((N_EXPERTS, n_tiles, 8, LANES), lambda c: (0, 0, 0, 0)),
                   pl.BlockSpec((RANK_CHUNK, N_EXPERTS), lambda c: (c, 0)),
                   pl.BlockSpec(memory_space=pltpu.SMEM)],
        out_shape=[jax.ShapeDtypeStruct((N_EXPERTS, n_tiles, 8, LANES), f32),
                   jax.ShapeDtypeStruct((t, N_EXPERTS), f32),
                   jax.ShapeDtypeStruct(((n_chunks + 1) * N_EXPERTS,), jnp.int32)],
        scratch_shapes=[pltpu.VMEM((N_EXPERTS, LANES), f32), pltpu.SMEM((N_EXPERTS,), jnp.int32)],
        compiler_params=_cparams(("arbitrary",)),
        name="rank",
    )(afft, thr, need)


DMA_UNROLL = 8
FFN_COL_BLOCKS = 4
FFN_SLOTS = 3


def _ffn_kernel(tok_ref, nxt_ref, far_ref, rows_ref, wg_ref, wu_ref, wd_ref, o_ref, xbuf, sem):
    s = pl.program_id(0)
    n = pl.num_programs(0) - 1
    slot = s % FFN_SLOTS
    ahead = (s + 2) % FFN_SLOTS
    tc = xbuf.shape[1]

    def row_copy(idx_ref, k, r):
        return pltpu.make_async_copy(rows_ref.at[pl.ds(idx_ref[0, 0, r], 1)], xbuf.at[k, pl.ds(r, 1)], sem.at[k])

    def wait_rows(k):
        pltpu.make_async_copy(rows_ref.at[pl.ds(0, tc)], xbuf.at[k], sem.at[k]).wait()

    @pl.when(s == 0)
    def _():
        def body(r, carry):
            row_copy(tok_ref, 0, r).start()
            row_copy(nxt_ref, 1, r).start()
            return carry
        lax.fori_loop(0, tc, body, 0, unroll=DMA_UNROLL)

    @pl.when(s < n)
    def _():
        wait_rows(slot)
        x = xbuf[slot].reshape(tc, D_MODEL).astype(bf16)
        per = tc // (2 * FFN_COL_BLOCKS)

        def issue(group):
            for r in range(group * per, (group + 1) * per):
                row_copy(far_ref, ahead, r).start()

        cw = EXPERT_FF // FFN_COL_BLOCKS
        mids = []
        for b in range(FFN_COL_BLOCKS):
            issue(b)
            g = jnp.dot(x, wg_ref[0, :, b * cw:(b + 1) * cw], preferred_element_type=f32)
            u = jnp.dot(x, wu_ref[0, :, b * cw:(b + 1) * cw], preferred_element_type=f32)
            mids.append((g * jax.nn.sigmoid(g) * u).astype(bf16))
        hmid = jnp.concatenate(mids, axis=1)
        ow = D_MODEL // FFN_COL_BLOCKS
        for b in range(FFN_COL_BLOCKS):
            issue(FFN_COL_BLOCKS + b)
            o_ref[:, b * ow:(b + 1) * ow] = jnp.dot(hmid, wd_ref[0, :, b * ow:(b + 1) * ow],
                                                    preferred_element_type=f32).astype(bf16)

    @pl.when(s == n)
    def _():
        wait_rows(slot)
        wait_rows((s + 1) % FFN_SLOTS)
        o_ref[...] = jnp.zeros_like(o_ref)


def _ffn(tok, rows, wg, wu, wd, capacity, tc):
    nt = capacity // tc
    n = N_EXPERTS * nt
    expert = lambda s: (jnp.minimum(s // nt, N_EXPERTS - 1), 0, 0)
    return pl.pallas_call(
        _ffn_kernel,
        grid=(n + 1,),
        in_specs=[pl.BlockSpec((1, 1, tc), lambda s: (jnp.minimum(s, n - 1), 0, 0), memory_space=pltpu.SMEM),
                  pl.BlockSpec((1, 1, tc), lambda s: (jnp.minimum(s + 1, n - 1), 0, 0), memory_space=pltpu.SMEM),
                  pl.BlockSpec((1, 1, tc), lambda s: (jnp.minimum(s + 2, n - 1), 0, 0), memory_space=pltpu.SMEM),
                  pl.BlockSpec(memory_space=pl.ANY),
                  pl.BlockSpec((1, D_MODEL, EXPERT_FF), expert), pl.BlockSpec((1, D_MODEL, EXPERT_FF), expert),
                  pl.BlockSpec((1, EXPERT_FF, D_MODEL), expert)],
        out_specs=pl.BlockSpec((tc, D_MODEL), lambda s: (s, 0)),
        out_shape=jax.ShapeDtypeStruct(((n + 1) * tc, D_MODEL), bf16),
        scratch_shapes=[pltpu.VMEM((FFN_SLOTS, tc, ROW_TILES, LANES), f32), pltpu.SemaphoreType.DMA((FFN_SLOTS,))],
        compiler_params=_cparams(("arbitrary",)),
        name="ffn",
    )(tok, tok, tok, rows, wg, wu, wd)


WIN = 128
ROW_ALIGN = 16


def _combine_kernel(offs_ref, h_ref, rank_ref, aff_ref, ye_ref, y_ref, win, sem, *, capacity):
    c = pl.program_id(0)
    n_chunks = pl.num_programs(0)
    slot = c % 2
    width = N_EXPERTS * WIN

    def window_starts(chunk):
        return [(offs_ref[chunk * N_EXPERTS + e] // ROW_ALIGN) * ROW_ALIGN for e in range(N_EXPERTS)]

    def copy(e, start, p, k):
        src = jnp.minimum(e * capacity + start + p * WIN, ye_ref.shape[0] - WIN)
        return pltpu.make_async_copy(ye_ref.at[pl.ds(pl.multiple_of(src, ROW_ALIGN), WIN)],
                                     win.at[k, pl.ds(e * WIN, WIN)], sem.at[k])

    starts = window_starts(c)
    n_pass = jnp.int32(0)
    for e in range(N_EXPERTS):
        st = offs_ref[c * N_EXPERTS + e]
        cnt = offs_ref[(c + 1) * N_EXPERTS + e] - st
        n_pass = jnp.maximum(n_pass, jnp.where(cnt > 0, (st - starts[e] + cnt + WIN - 1) // WIN, 0))

    @pl.when(c == 0)
    def _():
        for e in range(N_EXPERTS):
            copy(e, starts[e], 0, slot).start()

    @pl.when(c + 1 < n_chunks)
    def _():
        nxt = window_starts(c + 1)
        for e in range(N_EXPERTS):
            copy(e, nxt[e], 0, 1 - slot).start()

    col = lax.broadcasted_iota(jnp.int32, (N_EXPERTS, width), 1)
    row = lax.broadcasted_iota(jnp.int32, (N_EXPERTS, width), 0)
    spread = jnp.where((col // WIN) == row, 1.0, 0.0).astype(bf16)
    rank_rep = jnp.dot(rank_ref[...].astype(bf16), spread, preferred_element_type=f32)
    gate_rep = jnp.dot(aff_ref[...].astype(bf16), spread, preferred_element_type=f32)
    col1 = lax.broadcasted_iota(jnp.int32, (1, width), 1)
    base = jnp.zeros((1, width), jnp.int32)
    for e in range(N_EXPERTS):
        base = jnp.where((col1 // WIN) == e, starts[e] - offs_ref[c * N_EXPERTS + e], base)
    base = (base + col1 % WIN).astype(f32)

    def scatter_matmul(p, k):
        hit = rank_rep == base + (p * WIN).astype(f32)
        sel = jnp.where(hit, gate_rep, 0.0).astype(bf16)
        return jnp.dot(sel, win[k], preferred_element_type=f32)

    def wait_windows(k):
        for e in range(N_EXPERTS):
            copy(e, starts[e], 0, k).wait()

    wait_windows(slot)
    acc = scatter_matmul(jnp.int32(0), slot)

    def later_pass(p, acc):
        for e in range(N_EXPERTS):
            copy(e, starts[e], p, slot).start()
        wait_windows(slot)
        return acc + scatter_matmul(p, slot)

    y_ref[...] = h_ref[...] + lax.fori_loop(1, n_pass, later_pass, acc)


def _combine(offs, h, rank16, aff16, ye, capacity):
    t = h.shape[0]
    row = lambda c, offs: (c, 0)
    return pl.pallas_call(
        functools.partial(_combine_kernel, capacity=capacity),
        grid_spec=pltpu.PrefetchScalarGridSpec(
            num_scalar_prefetch=1,
            grid=(t // RANK_CHUNK,),
            in_specs=[pl.BlockSpec((RANK_CHUNK, D_MODEL), row), pl.BlockSpec((RANK_CHUNK, N_EXPERTS), row),
                      pl.BlockSpec((RANK_CHUNK, N_EXPERTS), row), pl.BlockSpec(memory_space=pl.ANY)],
            out_specs=pl.BlockSpec((RANK_CHUNK, D_MODEL), row),
            scratch_shapes=[pltpu.VMEM((2, N_EXPERTS * WIN, D_MODEL), bf16), pltpu.SemaphoreType.DMA((2,))]),
        out_shape=jax.ShapeDtypeStruct((t, D_MODEL), f32),
        compiler_params=_cparams(("arbitrary",)),
        name="combine",
    )(offs, h, rank16, aff16, ye)


def _rope_tables(seq):
    half = HEAD_DIM // 2
    inv = 1.0 / (ROPE_THETA ** (jnp.arange(0, HEAD_DIM, 2, dtype=f32) / HEAD_DIM))
    ang = jnp.arange(seq, dtype=f32)[:, None] * inv[None, :]
    cos = jnp.tile(jnp.cos(ang), (1, LANES // half))
    sign = jnp.where((jnp.arange(LANES) % HEAD_DIM) < half, -1.0, 1.0).astype(f32)
    sin = jnp.tile(jnp.sin(ang), (1, LANES // half)) * sign[None, :]
    return cos, sin


def _block_diag_mean(width):
    i = jnp.arange(width)
    return jnp.where((i[:, None] // HEAD_DIM) == (i[None, :] // HEAD_DIM), 1.0 / HEAD_DIM, 0.0).astype(bf16)


def _pick_tile(n, pref):
    while n % pref:
        pref //= 2
    return pref


def _trunk(x, p):
    batch, seq, _ = x.shape
    t = batch * seq
    capacity = CAPACITY_FACTOR * t // N_EXPERTS
    assert seq % BLOCK == 0 and t % RANK_CHUNK == 0 and capacity % LANES == 0
    tb = _pick_tile(seq, 512)
    tq = _pick_tile(seq, 256)
    kc = _pick_tile(seq, DIFF_KEY_CHUNK)
    heads = DIFF_HEADS if seq <= 2048 else DIFF_HEADS // 2
    tc = _pick_tile(capacity, FFN_TILE)
    cos, sin = _rope_tables(seq)
    gm512, gm128 = _block_diag_mean(512), _block_diag_mean(128)
    x2 = x.reshape(t, D_MODEL)
    depth = p["w_in"].shape[0]
    for l in range(depth):
        lam_init = 0.8 - 0.6 * math.exp(-0.3 * l)
        lam = (jnp.exp(jnp.sum(p["lambda_q1"][l] * p["lambda_k1"][l]))
               - jnp.exp(jnp.sum(p["lambda_q2"][l] * p["lambda_k2"][l])) + lam_init)
        bound = (8.0 * LOG2E * (1.0 + 2.0 ** -6)) * jnp.max(jnp.abs(p["diff_q_norm"][l])) \
            * jnp.max(jnp.abs(p["diff_k_norm"][l]))
        scal = jnp.stack([lam, jnp.asarray(1.0 - lam_init, f32), bound]).astype(f32)
        tile8 = lambda v: jnp.tile(v, 8)[None, :]
        dq, dk, dv, sqp, sk, sv = _attn_in(
            x2, p["attn_norm"][l][None, :], p["w_in_bf"][l], cos, sin,
            tile8(p["diff_q_norm"][l]), tile8(p["diff_k_norm"][l]), tile8(p["swa_q_norm"][l]),
            jnp.tile(p["swa_k_norm"][l], 2)[None, :], gm512, gm128, seq, tb)
        do = _diff_attn(scal, dq, dk, dv, p["diff_subln"][l][None, :], batch, seq, tq, heads, kc)
        so = _band_attn(p["swa_sink"][l], sqp, sk, sv, batch, seq)
        h, rows, aff16, afft = _attn_out(do, so, x2, p["w_out_bf"][l], p["ffn_norm"][l][None, :],
                                   jnp.concatenate([p["r_hi"][l], p["r_lo"][l]], axis=1), p["r_hi"][l], tb)
        sel = _select(aff16.reshape(t // 8, LANES), capacity)
        tokbuf, rank16, offs = _rank(afft, sel[0, :N_EXPERTS, None], sel[1, :N_EXPERTS, None], capacity)
        tok = tokbuf[:, :capacity // LANES, 0, :].reshape(N_EXPERTS * capacity // tc, 1, tc).astype(jnp.int32)
        ye = _ffn(tok, rows, p["w_gate_bf"][l], p["w_up_bf"][l], p["w_down_bf"][l], capacity, tc)
        x2 = _combine(offs, h, rank16, aff16, ye, capacity)
    return x2.reshape(batch, seq, D_MODEL)


def kernel(x_prompt, x_sample, attn_norm, w_in, diff_q_norm, diff_k_norm, lambda_q1, lambda_k1, lambda_q2, lambda_k2,
           diff_subln, swa_q_norm, swa_k_norm, swa_sink, w_out, ffn_norm, w_router, w_gate, w_up, w_down):
    r_pad = jnp.pad(w_router, ((0, 0), (0, 0), (0, LANES - N_EXPERTS)))
    r_hi = r_pad.astype(bf16)
    p = dict(attn_norm=attn_norm, diff_q_norm=diff_q_norm, diff_k_norm=diff_k_norm, lambda_q1=lambda_q1,
             lambda_k1=lambda_k1, lambda_q2=lambda_q2, lambda_k2=lambda_k2, diff_subln=diff_subln,
             swa_q_norm=swa_q_norm, swa_k_norm=swa_k_norm, swa_sink=swa_sink, ffn_norm=ffn_norm,
             w_in=w_in, w_in_bf=w_in.astype(bf16), w_out_bf=w_out.astype(bf16),
             r_hi=r_hi, r_lo=(r_pad - r_hi.astype(f32)).astype(bf16),
             w_gate_bf=w_gate.astype(bf16), w_up_bf=w_up.astype(bf16), w_down_bf=w_down.astype(bf16))
    return (_trunk(x_prompt, p), _trunk(x_sample, p))
```

```python
import functools
import math

import jax
import jax.numpy as jnp
from jax import lax
from jax.experimental import pallas as pl
from jax.experimental.pallas import tpu as pltpu

D_MODEL = 1024
HEAD_DIM = 64
DIFF_HEADS = 4
DIFF_VDIM = 128
DIFF_WIDTH = 512
SWA_HEADS = 8
SWA_KV_HEADS = 2
SWA_GROUP = 4
SWA_WIDTH = 512
BLOCK = 128
N_EXPERTS = 16
EXPERT_FF = 1024
CAPACITY_FACTOR = 2
ROPE_THETA = 10000.0
EPS = 1e-6
LOG2E = 1.4426950408889634
EXP_SHIFT_LIMIT = 60.0
DIFF_KEY_CHUNK = 256
IN_WIDTH = 2304
LANES = 128
ROW_TILES = D_MODEL // LANES
RANK_CHUNK = 256
NOT_MEMBER = -256.0
FFN_TILE = 512
VMEM_LIMIT = 56 * 1024 * 1024

f32 = jnp.float32
bf16 = jnp.bfloat16


def _cparams(sem):
    return pltpu.CompilerParams(dimension_semantics=sem, vmem_limit_bytes=VMEM_LIMIT)


def _lane_iota(shape):
    return lax.broadcasted_iota(jnp.int32, shape, len(shape) - 1)


def _head_norm_rope(u, gain, gmat, cos, sin_signed):
    w = u.shape[1]
    gw = gmat.shape[0]
    u2 = (u * u).astype(bf16)
    ms = jnp.concatenate([jnp.dot(u2[:, i:i + gw], gmat, preferred_element_type=f32) for i in range(0, w, gw)], axis=1)
    un = u * lax.rsqrt(ms + EPS) * gain
    lane = _lane_iota(un.shape)
    first_half = (lane % HEAD_DIM) < (HEAD_DIM // 2)
    rot = jnp.where(first_half, pltpu.roll(un, w - HEAD_DIM // 2, 1), pltpu.roll(un, HEAD_DIM // 2, 1))
    return un * cos + rot * sin_signed


def _attn_in_kernel(x_ref, g_ref, w_ref, cos_ref, sin_ref, gdq_ref, gdk_ref, gsq_ref, gsk_ref, gm256_ref, gm128_ref,
                    dq_ref, dk_ref, dv_ref, sq_ref, sk_ref, sv_ref):
    x = x_ref[...]
    ms = jnp.mean(x * x, axis=-1, keepdims=True)
    xn = (x * lax.rsqrt(ms + EPS) * g_ref[...]).astype(bf16)
    proj = lambda lo, hi: jnp.dot(xn, w_ref[:, lo:hi], preferred_element_type=f32)
    cos1, sin1 = cos_ref[...], sin_ref[...]
    cos4 = jnp.concatenate([cos1] * 4, axis=1)
    sin4 = jnp.concatenate([sin1] * 4, axis=1)
    gm256 = gm256_ref[...]
    scale = (1.0 / math.sqrt(HEAD_DIM)) * LOG2E
    dq = _head_norm_rope(proj(0, 512), gdq_ref[...], gm256, cos4, sin4)
    dk = _head_norm_rope(proj(512, 1024), gdk_ref[...], gm256, cos4, sin4)
    sq = _head_norm_rope(proj(1536, 2048), gsq_ref[...], gm256, cos4, sin4)
    skv = proj(2048, 2304)
    sk = _head_norm_rope(skv[:, 0:128], gsk_ref[...], gm128_ref[...], cos1, sin1)
    dq_ref[...] = (dq * scale).astype(bf16)
    dk_ref[...] = dk.astype(bf16)
    dv_ref[...] = proj(1024, 1536).astype(bf16)
    sk_ref[...] = sk.astype(bf16)
    sv_ref[...] = skv[:, 128:256].astype(bf16)
    sq = sq * scale
    lane = _lane_iota((sq.shape[0], LANES))
    for j in range(SWA_HEADS):
        blk = sq[:, (j // 2) * LANES:(j // 2 + 1) * LANES]
        hk = j // SWA_GROUP
        if (j % 2) != hk:
            blk = pltpu.roll(blk, HEAD_DIM, 1)
        blk = jnp.where((lane // HEAD_DIM) == hk, blk, 0.0)
        sq_ref[:, j * LANES:(j + 1) * LANES] = blk.astype(bf16)


def _attn_in(x2, g, w_in, cos, sin, gdq, gdk, gsq, gsk, gm256, gm128, seq, tb):
    t = x2.shape[0]
    nb_seq = seq // tb
    row = lambda i: (i, 0)
    const = lambda i: (0, 0)
    outs = [jax.ShapeDtypeStruct((t, 512), bf16)] * 3 + [jax.ShapeDtypeStruct((t, 1024), bf16)] + \
           [jax.ShapeDtypeStruct((t, 128), bf16)] * 2
    return pl.pallas_call(
        _attn_in_kernel,
        grid=(t // tb,),
        in_specs=[
            pl.BlockSpec((tb, D_MODEL), row),
            pl.BlockSpec((1, D_MODEL), const),
            pl.BlockSpec((D_MODEL, IN_WIDTH), const),
            pl.BlockSpec((tb, LANES), lambda i: (i % nb_seq, 0)),
            pl.BlockSpec((tb, LANES), lambda i: (i % nb_seq, 0)),
            pl.BlockSpec((1, 512), const), pl.BlockSpec((1, 512), const), pl.BlockSpec((1, 512), const),
            pl.BlockSpec((1, 128), const),
            pl.BlockSpec((256, 256), const), pl.BlockSpec((128, 128), const),
        ],
        out_specs=[pl.BlockSpec((tb, 512), row)] * 3 + [pl.BlockSpec((tb, 1024), row)] + [pl.BlockSpec((tb, 128), row)] * 2,
        out_shape=outs,
        compiler_params=_cparams(("arbitrary",)),
        name="attn_in",
    )(x2, g, w_in, cos, sin, gdq, gdk, gsq, gsk, gm256, gm128)


def _diff_attn_kernel(scal_ref, q_ref, k_ref, v_ref, g_ref, o_ref, *, kc):
    lam = scal_ref[0]
    out_scale = scal_ref[1]
    bound = scal_ref[2]
    tq = q_ref.shape[0]
    n_heads = q_ref.shape[1] // LANES
    n_chunks = k_ref.shape[0] // kc
    lane = _lane_iota((tq, LANES))

    def stacked_q(h):
        q = q_ref[:, h * LANES:(h + 1) * LANES]
        zero = jnp.zeros_like(q)
        return jnp.concatenate([jnp.where(lane < HEAD_DIM, q, zero), jnp.where(lane >= HEAD_DIM, q, zero)], axis=0)

    def value_block(rows, h):
        vblk = v_ref[rows, h * LANES:(h + 1) * LANES]
        return jnp.concatenate([vblk, jnp.ones_like(vblk)], axis=1)

    def finish(ev, h):
        on = ev[:, 0:LANES] * (1.0 / ev[:, LANES:LANES + 1])
        o = on[:tq] - lam * on[tq:]
        ms = jnp.mean(o * o, axis=-1, keepdims=True)
        o_ref[:, h * LANES:(h + 1) * LANES] = ((o * lax.rsqrt(ms + EPS) * g_ref[...]) * out_scale).astype(bf16)

    @pl.when(bound <= EXP_SHIFT_LIMIT)
    def _():
        for h in range(n_heads):
            qs = stacked_q(h)
            ev = jnp.zeros((2 * tq, 2 * LANES), f32)
            for c in range(n_chunks):
                rows = slice(c * kc, (c + 1) * kc)
                s = lax.dot_general(qs, k_ref[rows, h * LANES:(h + 1) * LANES], (((1,), (1,)), ((), ())),
                                    preferred_element_type=f32)
                ev = ev + jnp.dot(jnp.exp2(s - bound).astype(bf16), value_block(rows, h), preferred_element_type=f32)
            finish(ev, h)

    @pl.when(bound > EXP_SHIFT_LIMIT)
    def _():
        for h in range(n_heads):
            qs = stacked_q(h)

            def chunk(c, carry, qs=qs, h=h):
                m, ev = carry
                rows = pl.ds(pl.multiple_of(c * kc, kc), kc)
                s = lax.dot_general(qs, k_ref[rows, h * LANES:(h + 1) * LANES], (((1,), (1,)), ((), ())),
                                    preferred_element_type=f32)
                m_new = jnp.maximum(m, jnp.max(s, axis=-1, keepdims=True))
                ev = jnp.exp2(m - m_new) * ev + jnp.dot(jnp.exp2(s - m_new).astype(bf16), value_block(rows, h),
                                                        preferred_element_type=f32)
                return m_new, ev

            init = (jnp.full((2 * tq, 1), -jnp.inf, f32), jnp.zeros((2 * tq, 2 * LANES), f32))
            finish(lax.fori_loop(0, n_chunks, chunk, init)[1], h)


def _diff_attn(scal, dq, dk, dv, g, batch, seq, tq, heads, kc):
    t = dq.shape[0]
    nq = seq // tq
    w = heads * LANES
    return pl.pallas_call(
        functools.partial(_diff_attn_kernel, kc=kc),
        grid=(batch, DIFF_HEADS // heads, nq),
        in_specs=[
            pl.BlockSpec(memory_space=pltpu.SMEM),
            pl.BlockSpec((tq, w), lambda b, h, i: (b * nq + i, h)),
            pl.BlockSpec((seq, w), lambda b, h, i: (b, h)),
            pl.BlockSpec((seq, w), lambda b, h, i: (b, h)),
            pl.BlockSpec((1, LANES), lambda b, h, i: (0, 0)),
        ],
        out_specs=pl.BlockSpec((tq, w), lambda b, h, i: (b * nq + i, h)),
        out_shape=jax.ShapeDtypeStruct((t, DIFF_WIDTH), bf16),
        compiler_params=_cparams(("arbitrary", "arbitrary", "arbitrary")),
        name="diff_attn",
    )(scal, dq, dk, dv, g)


def _band_attn_kernel(scal_ref, q_ref, kp_ref, kc_ref, kn_ref, vp_ref, vc_ref, vn_ref, o_ref):
    m_blk = pl.program_id(1)
    last = pl.num_programs(1) - 1
    bound = scal_ref[SWA_HEADS]
    kc, vc = kc_ref[...], vc_ref[...]
    r = lax.broadcasted_iota(jnp.int32, (BLOCK, 3 * BLOCK), 0)
    c = lax.broadcasted_iota(jnp.int32, (BLOCK, 3 * BLOCK), 1)
    band = (c >= r) & (c <= r + 2 * BLOCK)
    lane = _lane_iota((BLOCK, LANES))
    grp = lax.broadcasted_iota(jnp.int32, (SWA_GROUP * BLOCK, 1), 0) // BLOCK

    def run(streaming):
        halves = (
            (0, kp_ref[...], kc, vp_ref[...], vc, band & ((c >= BLOCK) | (m_blk > 0))),
            (1, kc, kn_ref[...], vc, vn_ref[...], band & ((c < 2 * BLOCK) | (m_blk < last))),
        )
        for half, ka, kb, va, vb, valid in halves:
            k3 = jnp.concatenate([ka, kb], axis=0)
            v3 = jnp.concatenate([va, vb], axis=0)
            v31 = jnp.concatenate([v3, jnp.ones_like(v3)], axis=1)
            valid4 = jnp.concatenate([valid] * SWA_GROUP, axis=0)
            rows = slice(half * BLOCK, (half + 1) * BLOCK)
            outs = []
            for hk in range(SWA_KV_HEADS):
                qs = jnp.concatenate([q_ref[rows, (hk * SWA_GROUP + g) * LANES:(hk * SWA_GROUP + g + 1) * LANES]
                                      for g in range(SWA_GROUP)], axis=0)
                s = lax.dot_general(qs, k3, (((1,), (1,)), ((), ())), preferred_element_type=f32)
                sink = jnp.zeros((SWA_GROUP * BLOCK, 1), f32)
                for g in range(SWA_GROUP):
                    sink = jnp.where(grp == g, scal_ref[hk * SWA_GROUP + g], sink)
                if streaming:
                    m = jnp.maximum(sink, bound)
                    e = jnp.where(valid4, jnp.exp2(s - m), 0.0)
                else:
                    s = jnp.where(valid4, s, -1e30)
                    m = jnp.maximum(jnp.max(s, axis=-1, keepdims=True), sink)
                    e = jnp.exp2(s - m)
                ev = jnp.dot(e.astype(bf16), v31, preferred_element_type=f32)
                o = ev[:, 0:LANES] * (1.0 / (ev[:, LANES:LANES + 1] + jnp.exp2(sink - m)))
                outs.extend(o[g * BLOCK:(g + 1) * BLOCK] for g in range(SWA_GROUP))
            for pc in range(SWA_HEADS // 2):
                hk = (2 * pc) // SWA_GROUP
                left, right = outs[2 * pc], outs[2 * pc + 1]
                if hk == 1:
                    left = pltpu.roll(left, HEAD_DIM, 1)
                else:
                    right = pltpu.roll(right, HEAD_DIM, 1)
                o_ref[rows, pc * LANES:(pc + 1) * LANES] = jnp.where(lane < HEAD_DIM, left, right).astype(bf16)

    @pl.when(bound <= EXP_SHIFT_LIMIT)
    def _():
        run(True)

    @pl.when(bound > EXP_SHIFT_LIMIT)
    def _():
        run(False)


def _band_attn(scal, sqp, sk, sv, batch, seq):
    t = sqp.shape[0]
    nb = seq // BLOCK
    nb2 = nb // 2
    prev = lambda b, m: (b * nb + jnp.maximum(2 * m - 1, 0), 0)
    pair = lambda b, m: (b * nb2 + m, 0)
    nxt = lambda b, m: (b * nb + jnp.minimum(2 * m + 2, nb - 1), 0)
    one = lambda f: pl.BlockSpec((BLOCK, LANES), f)
    two = pl.BlockSpec((2 * BLOCK, LANES), pair)
    return pl.pallas_call(
        _band_attn_kernel,
        grid=(batch, nb2),
        in_specs=[pl.BlockSpec(memory_space=pltpu.SMEM), pl.BlockSpec((2 * BLOCK, SWA_HEADS * LANES), pair),
                  one(prev), two, one(nxt), one(prev), two, one(nxt)],
        out_specs=pl.BlockSpec((2 * BLOCK, SWA_WIDTH), pair),
        out_shape=jax.ShapeDtypeStruct((t, SWA_WIDTH), bf16),
        compiler_params=_cparams(("arbitrary", "arbitrary")),
        name="band_attn",
    )(scal, sqp, sk, sk, sk, sv, sv, sv)


def _attn_out_kernel(do_ref, so_ref, x_ref, w_ref, g_ref, rw_ref, rhi_ref, h_ref, hn_ref, aff_ref, afft_ref):
    mix = jnp.concatenate([do_ref[...], so_ref[...]], axis=1)
    h = x_ref[...] + jnp.dot(mix, w_ref[...], preferred_element_type=f32)
    h_ref[...] = h
    ms = jnp.mean(h * h, axis=-1, keepdims=True)
    hn = h * lax.rsqrt(ms + EPS) * g_ref[...]
    hn_ref[...] = hn.reshape(hn.shape[0], ROW_TILES, LANES)
    xh = hn.astype(bf16)
    xl = (hn - xh.astype(f32)).astype(bf16)
    both = jnp.dot(xh, rw_ref[...], preferred_element_type=f32)
    logits = both[:, 0:LANES] + (both[:, LANES:2 * LANES] + jnp.dot(xl, rhi_ref[...], preferred_element_type=f32))
    lane = _lane_iota(logits.shape)
    live = lane < N_EXPERTS
    lm = jnp.where(live, logits, -1e30)
    m = jnp.max(lm, axis=-1, keepdims=True)
    e = jnp.where(live, jnp.exp(lm - m), 0.0)
    aff = e / jnp.sum(e, axis=-1, keepdims=True)
    aff_ref[...] = aff[:, 0:N_EXPERTS]
    afft_ref[...] = aff.T[0:N_EXPERTS, :]


def _attn_out(do, so, x2, w, g, rw, rhi, tb):
    t = x2.shape[0]
    row = lambda i: (i, 0)
    const = lambda i: (0, 0)
    return pl.pallas_call(
        _attn_out_kernel,
        grid=(t // tb,),
        in_specs=[pl.BlockSpec((tb, 512), row), pl.BlockSpec((tb, 512), row), pl.BlockSpec((tb, D_MODEL), row),
                  pl.BlockSpec((D_MODEL, D_MODEL), const),
                  pl.BlockSpec((1, D_MODEL), const), pl.BlockSpec((D_MODEL, 2 * LANES), const),
                  pl.BlockSpec((D_MODEL, LANES), const)],
        out_specs=[pl.BlockSpec((tb, D_MODEL), row), pl.BlockSpec((tb, ROW_TILES, LANES), lambda i: (i, 0, 0)),
                   pl.BlockSpec((tb, N_EXPERTS), row), pl.BlockSpec((N_EXPERTS, tb), lambda i: (0, i))],
        out_shape=[jax.ShapeDtypeStruct((t, D_MODEL), f32), jax.ShapeDtypeStruct((t, ROW_TILES, LANES), f32),
                   jax.ShapeDtypeStruct((t, N_EXPERTS), f32), jax.ShapeDtypeStruct((N_EXPERTS, t), f32)],
        compiler_params=_cparams(("arbitrary",)),
        name="attn_out",
    )(do, so, x2, w, g, rw, rhi)


def _class_sum(v):
    v = v + pltpu.roll(v, 64, 1)
    v = v + pltpu.roll(v, 32, 1)
    return v + pltpu.roll(v, 16, 1)


def _select_kernel(a_ref, o_ref, *, capacity):
    bits = pltpu.bitcast(a_ref[...], jnp.int32)

    def count_ge(cand):
        cnt = jnp.sum((bits >= cand[0:1, :]).astype(f32), axis=0, keepdims=True)
        return _class_sum(jnp.broadcast_to(cnt, (8, LANES)))

    def body(i, thr):
        cand = thr | jnp.left_shift(jnp.int32(1), 30 - i)
        return jnp.where(count_ge(cand) >= capacity, cand, thr)

    thr = lax.fori_loop(0, 31, body, jnp.zeros((8, LANES), jnp.int32))
    n_gt = count_ge(thr + 1)
    row = lax.broadcasted_iota(jnp.int32, (8, LANES), 0)
    o_ref[...] = jnp.where(row == 0, pltpu.bitcast(thr, f32), capacity - n_gt)


def _select(aff_dense, capacity):
    return pl.pallas_call(
        functools.partial(_select_kernel, capacity=float(capacity)),
        out_shape=jax.ShapeDtypeStruct((8, LANES), f32),
        compiler_params=pltpu.CompilerParams(vmem_limit_bytes=VMEM_LIMIT),
        name="select",
    )(aff_dense)


def _rank_kernel(afft_ref, thr_ref, need_ref, tok_ref, rank_ref, offs_out_ref, carry_ref, offs_ref):
    c = pl.program_id(0)

    @pl.when(c == 0)
    def _():
        tok_ref[...] = jnp.zeros_like(tok_ref)
        carry_ref[...] = jnp.zeros_like(carry_ref)
        for e in range(N_EXPERTS):
            offs_ref[e] = 0

    a = afft_ref[...]
    thr = thr_ref[...]
    gt = a > thr
    eq = a == thr
    ri = lax.broadcasted_iota(jnp.int32, (RANK_CHUNK, RANK_CHUNK), 0)
    ci = lax.broadcasted_iota(jnp.int32, (RANK_CHUNK, RANK_CHUNK), 1)
    before = (ri < ci).astype(bf16)
    eq_rank = jnp.dot(eq.astype(bf16), before, preferred_element_type=f32) + carry_ref[:, 0:1]
    sel = gt | (eq & (eq_rank < need_ref[...]))
    local = jnp.dot(sel.astype(bf16), before, preferred_element_type=f32)
    local = jnp.where(sel, local, NOT_MEMBER)
    rank_ref[...] = local.T
    n_sel = jnp.sum(sel.astype(f32), axis=1, keepdims=True)
    carry_ref[:, 0:1] = carry_ref[:, 0:1] + jnp.sum(eq.astype(f32), axis=1, keepdims=True)

    sub = lax.broadcasted_iota(jnp.int32, (8, RANK_CHUNK), 0)
    loc = lax.broadcasted_iota(jnp.int32, (8, RANK_CHUNK), 1)
    lhs = jnp.where(sub == 0, loc, jnp.where(sub == 1, 1, 0)).astype(f32).astype(bf16)
    slot = lax.broadcasted_iota(jnp.int32, (3 * LANES, RANK_CHUNK), 0).astype(f32)
    base = (c * RANK_CHUNK).astype(f32)
    for e in range(N_EXPERTS):
        off = offs_ref[e]
        j0 = off // LANES
        shift = (off - j0 * LANES).astype(f32)
        row_rank = local[e:e + 1, :]
        rel = jnp.where(row_rank >= 0.0, row_rank + shift, -1.0)
        onehot = jnp.where(slot == rel, 1.0, 0.0).astype(bf16)
        placed = lax.dot_general(lhs, onehot, (((1,), (1,)), ((), ())), preferred_element_type=f32)
        row = lax.broadcasted_iota(jnp.int32, placed.shape, 0)
        filled = jnp.broadcast_to(placed[1:2, :], placed.shape)
        placed = jnp.where(row == 0, placed + base * filled, placed)
        for k in range(3):
            tok_ref[e, j0 + k] = tok_ref[e, j0 + k] + placed[:, k * LANES:(k + 1) * LANES]
        new_off = off + jnp.sum(n_sel[e:e + 1, :]).astype(jnp.int32)
        offs_ref[e] = new_off
        offs_out_ref[c * N_EXPERTS + e] = off
        offs_out_ref[(c + 1) * N_EXPERTS + e] = new_off


def _rank(afft, thr, need, capacity):
    t = afft.shape[1]
    n_chunks = t // RANK_CHUNK
    n_tiles = capacity // LANES + 3
    const2 = lambda c: (0, 0)
    return pl.pallas_call(
        _rank_kernel,
        grid=(n_chunks,),
        in_specs=[pl.BlockSpec((N_EXPERTS, RANK_CHUNK), lambda c: (0, c)),
                  pl.BlockSpec((N_EXPERTS, 1), const2), pl.BlockSpec((N_EXPERTS, 1), const2)],
        out_specs=[pl.BlockSpec((N_EXPERTS, n_tiles, 8, LANES), lambda c: (0, 0, 0, 0)),
                   pl.BlockSpec((RANK_CHUNK, N_EXPERTS), lambda c: (c, 0)),
                   pl.BlockSpec(memory_space=pltpu.SMEM)],
        out_shape=[jax.ShapeDtypeStruct((N_EXPERTS, n_tiles, 8, LANES), f32),
                   jax.ShapeDtypeStruct((t, N_EXPERTS), f32),
                   jax.ShapeDtypeStruct(((n_chunks + 1) * N_EXPERTS,), jnp.int32)],
        scratch_shapes=[pltpu.VMEM((N_EXPERTS, LANES), f32), pltpu.SMEM((N_EXPERTS,), jnp.int32)],
        compiler_params=_cparams(("arbitrary",)),
        name="rank",
    )(afft, thr, need)


DMA_UNROLL = 8
FFN_COL_BLOCKS = 4
FFN_SLOTS = 3


def _ffn_kernel(tok_ref, nxt_ref, far_ref, rows_ref, wg_ref, wu_ref, wd_ref, o_ref, xbuf, sem):
    s = pl.program_id(0)
    n = pl.num_programs(0) - 1
    slot = s % FFN_SLOTS
    ahead = (s + 2) % FFN_SLOTS
    tc = xbuf.shape[1]

    def row_copy(idx_ref, k, r):
        return pltpu.make_async_copy(rows_ref.at[pl.ds(idx_ref[0, 0, r], 1)], xbuf.at[k, pl.ds(r, 1)], sem.at[k])

    def wait_rows(k):
        pltpu.make_async_copy(rows_ref.at[pl.ds(0, tc)], xbuf.at[k], sem.at[k]).wait()

    @pl.when(s == 0)
    def _():
        def body(r, carry):
            row_copy(tok_ref, 0, r).start()
            row_copy(nxt_ref, 1, r).start()
            return carry
        lax.fori_loop(0, tc, body, 0, unroll=DMA_UNROLL)

    @pl.when(s < n)
    def _():
        wait_rows(slot)
        x = xbuf[slot].reshape(tc, D_MODEL).astype(bf16)
        per = tc // (2 * FFN_COL_BLOCKS)

        def issue(group):
            for r in range(group * per, (group + 1) * per):
                row_copy(far_ref, ahead, r).start()

        cw = EXPERT_FF // FFN_COL_BLOCKS
        mids = []
        for b in range(FFN_COL_BLOCKS):
            issue(b)
            g = jnp.dot(x, wg_ref[0, :, b * cw:(b + 1) * cw], preferred_element_type=f32)
            u = jnp.dot(x, wu_ref[0, :, b * cw:(b + 1) * cw], preferred_element_type=f32)
            mids.append((g * jax.nn.sigmoid(g) * u).astype(bf16))
        hmid = jnp.concatenate(mids, axis=1)
        ow = D_MODEL // FFN_COL_BLOCKS
        for b in range(FFN_COL_BLOCKS):
            issue(FFN_COL_BLOCKS + b)
            o_ref[:, b * ow:(b + 1) * ow] = jnp.dot(hmid, wd_ref[0, :, b * ow:(b + 1) * ow],
                                                    preferred_element_type=f32).astype(bf16)

    @pl.when(s == n)
    def _():
        wait_rows(slot)
        wait_rows((s + 1) % FFN_SLOTS)
        o_ref[...] = jnp.zeros_like(o_ref)


def _ffn(tok, rows, wg, wu, wd, capacity, tc):
    nt = capacity // tc
    n = N_EXPERTS * nt
    expert = lambda s: (jnp.minimum(s // nt, N_EXPERTS - 1), 0, 0)
    return pl.pallas_call(
        _ffn_kernel,
        grid=(n + 1,),
        in_specs=[pl.BlockSpec((1, 1, tc), lambda s: (jnp.minimum(s, n - 1), 0, 0), memory_space=pltpu.SMEM),
                  pl.BlockSpec((1, 1, tc), lambda s: (jnp.minimum(s + 1, n - 1), 0, 0), memory_space=pltpu.SMEM),
                  pl.BlockSpec((1, 1, tc), lambda s: (jnp.minimum(s + 2, n - 1), 0, 0), memory_space=pltpu.SMEM),
                  pl.BlockSpec(memory_space=pl.ANY),
                  pl.BlockSpec((1, D_MODEL, EXPERT_FF), expert), pl.BlockSpec((1, D_MODEL, EXPERT_FF), expert),
                  pl.BlockSpec((1, EXPERT_FF, D_MODEL), expert)],
        out_specs=pl.BlockSpec((tc, D_MODEL), lambda s: (s, 0)),
        out_shape=jax.ShapeDtypeStruct(((n + 1) * tc, D_MODEL), bf16),
        scratch_shapes=[pltpu.VMEM((FFN_SLOTS, tc, ROW_TILES, LANES), f32), pltpu.SemaphoreType.DMA((FFN_SLOTS,))],
        compiler_params=_cparams(("arbitrary",)),
        name="ffn",
    )(tok, tok, tok, rows, wg, wu, wd)


WIN = 128
ROW_ALIGN = 16


def _combine_kernel(offs_ref, h_ref, rank_ref, aff_ref, ye_ref, y_ref, win, sem, *, capacity):
    c = pl.program_id(0)
    n_chunks = pl.num_programs(0)
    slot = c % 2
    width = N_EXPERTS * WIN

    def window_starts(chunk):
        return [(offs_ref[chunk * N_EXPERTS + e] // ROW_ALIGN) * ROW_ALIGN for e in range(N_EXPERTS)]

    def copy(e, start, p, k):
        src = jnp.minimum(e * capacity + start + p * WIN, ye_ref.shape[0] - WIN)
        return pltpu.make_async_copy(ye_ref.at[pl.ds(pl.multiple_of(src, ROW_ALIGN), WIN)],
                                     win.at[k, pl.ds(e * WIN, WIN)], sem.at[k])

    starts = window_starts(c)
    n_pass = jnp.int32(0)
    for e in range(N_EXPERTS):
        st = offs_ref[c * N_EXPERTS + e]
        cnt = offs_ref[(c + 1) * N_EXPERTS + e] - st
        n_pass = jnp.maximum(n_pass, jnp.where(cnt > 0, (st - starts[e] + cnt + WIN - 1) // WIN, 0))

    @pl.when(c == 0)
    def _():
        for e in range(N_EXPERTS):
            copy(e, starts[e], 0, slot).start()

    @pl.when(c + 1 < n_chunks)
    def _():
        nxt = window_starts(c + 1)
        for e in range(N_EXPERTS):
            copy(e, nxt[e], 0, 1 - slot).start()

    col = lax.broadcasted_iota(jnp.int32, (N_EXPERTS, width), 1)
    row = lax.broadcasted_iota(jnp.int32, (N_EXPERTS, width), 0)
    spread = jnp.where((col // WIN) == row, 1.0, 0.0).astype(bf16)
    rank_rep = jnp.dot(rank_ref[...].astype(bf16), spread, preferred_element_type=f32)
    gate_rep = jnp.dot(aff_ref[...].astype(bf16), spread, preferred_element_type=f32)
    col1 = lax.broadcasted_iota(jnp.int32, (1, width), 1)
    base = jnp.zeros((1, width), jnp.int32)
    for e in range(N_EXPERTS):
        base = jnp.where((col1 // WIN) == e, starts[e] - offs_ref[c * N_EXPERTS + e], base)
    base = (base + col1 % WIN).astype(f32)

    def scatter_matmul(p, k):
        hit = rank_rep == base + (p * WIN).astype(f32)
        sel = jnp.where(hit, gate_rep, 0.0).astype(bf16)
        return jnp.dot(sel, win[k], preferred_element_type=f32)

    def wait_windows(k):
        for e in range(N_EXPERTS):
            copy(e, starts[e], 0, k).wait()

    wait_windows(slot)
    acc = scatter_matmul(jnp.int32(0), slot)

    def later_pass(p, acc):
        for e in range(N_EXPERTS):
            copy(e, starts[e], p, slot).start()
        wait_windows(slot)
        return acc + scatter_matmul(p, slot)

    y_ref[...] = h_ref[...] + lax.fori_loop(1, n_pass, later_pass, acc)


def _combine(offs, h, rank16, aff16, ye, capacity):
    t = h.shape[0]
    row = lambda c, offs: (c, 0)
    return pl.pallas_call(
        functools.partial(_combine_kernel, capacity=capacity),
        grid_spec=pltpu.PrefetchScalarGridSpec(
            num_scalar_prefetch=1,
            grid=(t // RANK_CHUNK,),
            in_specs=[pl.BlockSpec((RANK_CHUNK, D_MODEL), row), pl.BlockSpec((RANK_CHUNK, N_EXPERTS), row),
                      pl.BlockSpec((RANK_CHUNK, N_EXPERTS), row), pl.BlockSpec(memory_space=pl.ANY)],
            out_specs=pl.BlockSpec((RANK_CHUNK, D_MODEL), row),
            scratch_shapes=[pltpu.VMEM((2, N_EXPERTS * WIN, D_MODEL), bf16), pltpu.SemaphoreType.DMA((2,))]),
        out_shape=jax.ShapeDtypeStruct((t, D_MODEL), f32),
        compiler_params=_cparams(("arbitrary",)),
        name="combine",
    )(offs, h, rank16, aff16, ye)


def _rope_tables(seq):
    half = HEAD_DIM // 2
    inv = 1.0 / (ROPE_THETA ** (jnp.arange(0, HEAD_DIM, 2, dtype=f32) / HEAD_DIM))
    ang = jnp.arange(seq, dtype=f32)[:, None] * inv[None, :]
    cos = jnp.tile(jnp.cos(ang), (1, LANES // half))
    sign = jnp.where((jnp.arange(LANES) % HEAD_DIM) < half, -1.0, 1.0).astype(f32)
    sin = jnp.tile(jnp.sin(ang), (1, LANES // half)) * sign[None, :]
    return cos, sin


def _block_diag_mean(width):
    i = jnp.arange(width)
    return jnp.where((i[:, None] // HEAD_DIM) == (i[None, :] // HEAD_DIM), 1.0 / HEAD_DIM, 0.0).astype(bf16)


def _pick_tile(n, pref):
    while n % pref:
        pref //= 2
    return pref


def _trunk(x, p):
    batch, seq, _ = x.shape
    t = batch * seq
    capacity = CAPACITY_FACTOR * t // N_EXPERTS
    assert seq % (2 * BLOCK) == 0 and t % RANK_CHUNK == 0 and capacity % LANES == 0
    tb = _pick_tile(seq, 512)
    tq = _pick_tile(seq, 256)
    kc = _pick_tile(seq, DIFF_KEY_CHUNK)
    heads = DIFF_HEADS if seq <= 2048 else DIFF_HEADS // 2
    tc = _pick_tile(capacity, FFN_TILE)
    cos, sin = _rope_tables(seq)
    gm256, gm128 = _block_diag_mean(256), _block_diag_mean(128)
    x2 = x.reshape(t, D_MODEL)
    depth = p["w_in"].shape[0]
    for l in range(depth):
        lam_init = 0.8 - 0.6 * math.exp(-0.3 * l)
        lam = (jnp.exp(jnp.sum(p["lambda_q1"][l] * p["lambda_k1"][l]))
               - jnp.exp(jnp.sum(p["lambda_q2"][l] * p["lambda_k2"][l])) + lam_init)
        bound = (8.0 * LOG2E * (1.0 + 2.0 ** -6)) * jnp.max(jnp.abs(p["diff_q_norm"][l])) \
            * jnp.max(jnp.abs(p["diff_k_norm"][l]))
        scal = jnp.stack([lam, jnp.asarray(1.0 - lam_init, f32), bound]).astype(f32)
        tile8 = lambda v: jnp.tile(v, 8)[None, :]
        dq, dk, dv, sqp, sk, sv = _attn_in(
            x2, p["attn_norm"][l][None, :], p["w_in_bf"][l], cos, sin,
            tile8(p["diff_q_norm"][l]), tile8(p["diff_k_norm"][l]), tile8(p["swa_q_norm"][l]),
            jnp.tile(p["swa_k_norm"][l], 2)[None, :], gm256, gm128, seq, tb)
        do = _diff_attn(scal, dq, dk, dv, p["diff_subln"][l][None, :], batch, seq, tq, heads, kc)
        swa_bound = (8.0 * LOG2E * (1.0 + 2.0 ** -6)) * jnp.max(jnp.abs(p["swa_q_norm"][l])) \
            * jnp.max(jnp.abs(p["swa_k_norm"][l]))
        so = _band_attn(jnp.concatenate([p["swa_sink"][l] * LOG2E, swa_bound[None]]).astype(f32), sqp, sk, sv, batch, seq)
        h, rows, aff16, afft = _attn_out(do, so, x2, p["w_out_bf"][l], p["ffn_norm"][l][None, :],
                                   jnp.concatenate([p["r_hi"][l], p["r_lo"][l]], axis=1), p["r_hi"][l], tb)
        sel = _select(aff16.reshape(t // 8, LANES), capacity)
        tokbuf, rank16, offs = _rank(afft, sel[0, :N_EXPERTS, None], sel[1, :N_EXPERTS, None], capacity)
        tok = tokbuf[:, :capacity // LANES, 0, :].reshape(N_EXPERTS * capacity // tc, 1, tc).astype(jnp.int32)
        ye = _ffn(tok, rows, p["w_gate_bf"][l], p["w_up_bf"][l], p["w_down_bf"][l], capacity, tc)
        x2 = _combine(offs, h, rank16, aff16, ye, capacity)
    return x2.reshape(batch, seq, D_MODEL)


def kernel(x_prompt, x_sample, attn_norm, w_in, diff_q_norm, diff_k_norm, lambda_q1, lambda_k1, lambda_q2, lambda_k2,
           diff_subln, swa_q_norm, swa_k_norm, swa_sink, w_out, ffn_norm, w_router, w_gate, w_up, w_down):
    r_pad = jnp.pad(w_router, ((0, 0), (0, 0), (0, LANES - N_EXPERTS)))
    r_hi = r_pad.astype(bf16)
    p = dict(attn_norm=attn_norm, diff_q_norm=diff_q_norm, diff_k_norm=diff_k_norm, lambda_q1=lambda_q1,
             lambda_k1=lambda_k1, lambda_q2=lambda_q2, lambda_k2=lambda_k2, diff_subln=diff_subln,
             swa_q_norm=swa_q_norm, swa_k_norm=swa_k_norm, swa_sink=swa_sink, ffn_norm=ffn_norm,
             w_in=w_in, w_in_bf=w_in.astype(bf16), w_out_bf=w_out.astype(bf16),
             r_hi=r_hi, r_lo=(r_pad - r_hi.astype(f32)).astype(bf16),
             w_gate_bf=w_gate.astype(bf16), w_up_bf=w_up.astype(bf16), w_down_bf=w_down.astype(bf16))
    return (_trunk(x_prompt, p), _trunk(x_sample, p))
```

```python
import functools
import math

import jax
import jax.numpy as jnp
from jax import lax
from jax.experimental import pallas as pl
from jax.experimental.pallas import tpu as pltpu

D_MODEL = 1024
HEAD_DIM = 64
DIFF_HEADS = 4
DIFF_VDIM = 128
DIFF_WIDTH = 512
SWA_HEADS = 8
SWA_KV_HEADS = 2
SWA_GROUP = 4
SWA_WIDTH = 512
BLOCK = 128
N_EXPERTS = 16
EXPERT_FF = 1024
CAPACITY_FACTOR = 2
ROPE_THETA = 10000.0
EPS = 1e-6
LOG2E = 1.4426950408889634
EXP_SHIFT_LIMIT = 60.0
DIFF_KEY_CHUNK = 256
IN_WIDTH = 2304
LANES = 128
ROW_TILES = D_MODEL // LANES
RANK_CHUNK = 256
NOT_MEMBER = -256.0
FFN_TILE = 512
VMEM_LIMIT = 56 * 1024 * 1024

f32 = jnp.float32
bf16 = jnp.bfloat16


def _cparams(sem):
    return pltpu.CompilerParams(dimension_semantics=sem, vmem_limit_bytes=VMEM_LIMIT)


def _lane_iota(shape):
    return lax.broadcasted_iota(jnp.int32, shape, len(shape) - 1)


def _head_norm_rope(u, gain, gmat, cos, sin_signed):
    w = u.shape[1]
    gw = gmat.shape[0]
    u2 = (u * u).astype(bf16)
    ms = jnp.concatenate([jnp.dot(u2[:, i:i + gw], gmat, preferred_element_type=f32) for i in range(0, w, gw)], axis=1)
    un = u * lax.rsqrt(ms + EPS) * gain
    lane = _lane_iota(un.shape)
    first_half = (lane % HEAD_DIM) < (HEAD_DIM // 2)
    rot = jnp.where(first_half, pltpu.roll(un, w - HEAD_DIM // 2, 1), pltpu.roll(un, HEAD_DIM // 2, 1))
    return un * cos + rot * sin_signed


def _attn_in_kernel(x_ref, g_ref, w_ref, cos_ref, sin_ref, gdq_ref, gdk_ref, gsq_ref, gsk_ref, gm256_ref, gm128_ref,
                    dq_ref, dk_ref, dv_ref, sq_ref, sk_ref, sv_ref):
    x = x_ref[...]
    ms = jnp.mean(x * x, axis=-1, keepdims=True)
    xn = (x * lax.rsqrt(ms + EPS) * g_ref[...]).astype(bf16)
    proj = lambda lo, hi: jnp.dot(xn, w_ref[:, lo:hi], preferred_element_type=f32)
    cos1, sin1 = cos_ref[...], sin_ref[...]
    cos4 = jnp.concatenate([cos1] * 4, axis=1)
    sin4 = jnp.concatenate([sin1] * 4, axis=1)
    gm256 = gm256_ref[...]
    scale = (1.0 / math.sqrt(HEAD_DIM)) * LOG2E
    dq = _head_norm_rope(proj(0, 512), gdq_ref[...], gm256, cos4, sin4)
    dk = _head_norm_rope(proj(512, 1024), gdk_ref[...], gm256, cos4, sin4)
    sq = _head_norm_rope(proj(1536, 2048), gsq_ref[...], gm256, cos4, sin4)
    skv = proj(2048, 2304)
    sk = _head_norm_rope(skv[:, 0:128], gsk_ref[...], gm128_ref[...], cos1, sin1)
    dq_ref[...] = (dq * scale).astype(bf16)
    dk_ref[...] = dk.astype(bf16)
    dv_ref[...] = proj(1024, 1536).astype(bf16)
    sk_ref[...] = sk.astype(bf16)
    sv_ref[...] = skv[:, 128:256].astype(bf16)
    sq = sq * scale
    lane = _lane_iota((sq.shape[0], LANES))
    for j in range(SWA_HEADS):
        blk = sq[:, (j // 2) * LANES:(j // 2 + 1) * LANES]
        hk = j // SWA_GROUP
        if (j % 2) != hk:
            blk = pltpu.roll(blk, HEAD_DIM, 1)
        blk = jnp.where((lane // HEAD_DIM) == hk, blk, 0.0)
        sq_ref[:, j * LANES:(j + 1) * LANES] = blk.astype(bf16)


def _attn_in(x2, g, w_in, cos, sin, gdq, gdk, gsq, gsk, gm256, gm128, seq, tb):
    t = x2.shape[0]
    nb_seq = seq // tb
    row = lambda i: (i, 0)
    const = lambda i: (0, 0)
    outs = [jax.ShapeDtypeStruct((t, 512), bf16)] * 3 + [jax.ShapeDtypeStruct((t, 1024), bf16)] + \
           [jax.ShapeDtypeStruct((t, 128), bf16)] * 2
    return pl.pallas_call(
        _attn_in_kernel,
        grid=(t // tb,),
        in_specs=[
            pl.BlockSpec((tb, D_MODEL), row),
            pl.BlockSpec((1, D_MODEL), const),
            pl.BlockSpec((D_MODEL, IN_WIDTH), const),
            pl.BlockSpec((tb, LANES), lambda i: (i % nb_seq, 0)),
            pl.BlockSpec((tb, LANES), lambda i: (i % nb_seq, 0)),
            pl.BlockSpec((1, 512), const), pl.BlockSpec((1, 512), const), pl.BlockSpec((1, 512), const),
            pl.BlockSpec((1, 128), const),
            pl.BlockSpec((256, 256), const), pl.BlockSpec((128, 128), const),
        ],
        out_specs=[pl.BlockSpec((tb, 512), row)] * 3 + [pl.BlockSpec((tb, 1024), row)] + [pl.BlockSpec((tb, 128), row)] * 2,
        out_shape=outs,
        compiler_params=_cparams(("arbitrary",)),
        name="attn_in",
    )(x2, g, w_in, cos, sin, gdq, gdk, gsq, gsk, gm256, gm128)


def _diff_attn_kernel(scal_ref, q_ref, k_ref, v_ref, g_ref, o_ref, *, kc):
    lam = scal_ref[0]
    out_scale = scal_ref[1]
    bound = scal_ref[2]
    tq = q_ref.shape[0]
    n_heads = q_ref.shape[1] // LANES
    n_chunks = k_ref.shape[0] // kc
    lane = _lane_iota((tq, LANES))

    def stacked_q(h):
        q = q_ref[:, h * LANES:(h + 1) * LANES]
        zero = jnp.zeros_like(q)
        return jnp.concatenate([jnp.where(lane < HEAD_DIM, q, zero), jnp.where(lane >= HEAD_DIM, q, zero)], axis=0)

    def value_block(rows, h):
        vblk = v_ref[rows, h * LANES:(h + 1) * LANES]
        return jnp.concatenate([vblk, jnp.ones_like(vblk)], axis=1)

    def finish(ev, h):
        on = ev[:, 0:LANES] * (1.0 / ev[:, LANES:LANES + 1])
        o = on[:tq] - lam * on[tq:]
        ms = jnp.mean(o * o, axis=-1, keepdims=True)
        o_ref[:, h * LANES:(h + 1) * LANES] = ((o * lax.rsqrt(ms + EPS) * g_ref[...]) * out_scale).astype(bf16)

    @pl.when(bound <= EXP_SHIFT_LIMIT)
    def _():
        for h in range(n_heads):
            qs = stacked_q(h)
            ev = jnp.zeros((2 * tq, 2 * LANES), f32)
            for c in range(n_chunks):
                rows = slice(c * kc, (c + 1) * kc)
                s = lax.dot_general(qs, k_ref[rows, h * LANES:(h + 1) * LANES], (((1,), (1,)), ((), ())),
                                    preferred_element_type=f32)
                ev = ev + jnp.dot(jnp.exp2(s - bound).astype(bf16), value_block(rows, h), preferred_element_type=f32)
            finish(ev, h)

    @pl.when(bound > EXP_SHIFT_LIMIT)
    def _():
        for h in range(n_heads):
            qs = stacked_q(h)

            def chunk(c, carry, qs=qs, h=h):
                m, ev = carry
                rows = pl.ds(pl.multiple_of(c * kc, kc), kc)
                s = lax.dot_general(qs, k_ref[rows, h * LANES:(h + 1) * LANES], (((1,), (1,)), ((), ())),
                                    preferred_element_type=f32)
                m_new = jnp.maximum(m, jnp.max(s, axis=-1, keepdims=True))
                ev = jnp.exp2(m - m_new) * ev + jnp.dot(jnp.exp2(s - m_new).astype(bf16), value_block(rows, h),
                                                        preferred_element_type=f32)
                return m_new, ev

            init = (jnp.full((2 * tq, 1), -jnp.inf, f32), jnp.zeros((2 * tq, 2 * LANES), f32))
            finish(lax.fori_loop(0, n_chunks, chunk, init)[1], h)


def _diff_attn(scal, dq, dk, dv, g, batch, seq, tq, heads, kc):
    t = dq.shape[0]
    nq = seq // tq
    w = heads * LANES
    return pl.pallas_call(
        functools.partial(_diff_attn_kernel, kc=kc),
        grid=(batch, DIFF_HEADS // heads, nq),
        in_specs=[
            pl.BlockSpec(memory_space=pltpu.SMEM),
            pl.BlockSpec((tq, w), lambda b, h, i: (b * nq + i, h)),
            pl.BlockSpec((seq, w), lambda b, h, i: (b, h)),
            pl.BlockSpec((seq, w), lambda b, h, i: (b, h)),
            pl.BlockSpec((1, LANES), lambda b, h, i: (0, 0)),
        ],
        out_specs=pl.BlockSpec((tq, w), lambda b, h, i: (b * nq + i, h)),
        out_shape=jax.ShapeDtypeStruct((t, DIFF_WIDTH), bf16),
        compiler_params=_cparams(("arbitrary", "arbitrary", "arbitrary")),
        name="diff_attn",
    )(scal, dq, dk, dv, g)


def _band_attn_kernel(scal_ref, q_ref, kp_ref, kc_ref, kn_ref, vp_ref, vc_ref, vn_ref, o_ref):
    m_blk = pl.program_id(1)
    last = pl.num_programs(1) - 1
    bound = scal_ref[SWA_HEADS]
    kc, vc = kc_ref[...], vc_ref[...]
    r = lax.broadcasted_iota(jnp.int32, (BLOCK, 3 * BLOCK), 0)
    c = lax.broadcasted_iota(jnp.int32, (BLOCK, 3 * BLOCK), 1)
    band = (c >= r) & (c <= r + 2 * BLOCK)
    lane = _lane_iota((BLOCK, LANES))
    grp = lax.broadcasted_iota(jnp.int32, (SWA_GROUP * BLOCK, 1), 0) // BLOCK

    def run(streaming):
        halves = (
            (0, kp_ref[...], kc, vp_ref[...], vc, band & ((c >= BLOCK) | (m_blk > 0))),
            (1, kc, kn_ref[...], vc, vn_ref[...], band & ((c < 2 * BLOCK) | (m_blk < last))),
        )
        for half, ka, kb, va, vb, valid in halves:
            k3 = jnp.concatenate([ka, kb], axis=0)
            v3 = jnp.concatenate([va, vb], axis=0)
            v31 = jnp.concatenate([v3, jnp.ones_like(v3)], axis=1)
            valid4 = jnp.concatenate([valid] * SWA_GROUP, axis=0)
            rows = slice(half * BLOCK, (half + 1) * BLOCK)
            outs = []
            for hk in range(SWA_KV_HEADS):
                qs = jnp.concatenate([q_ref[rows, (hk * SWA_GROUP + g) * LANES:(hk * SWA_GROUP + g + 1) * LANES]
                                      for g in range(SWA_GROUP)], axis=0)
                s = lax.dot_general(qs, k3, (((1,), (1,)), ((), ())), preferred_element_type=f32)
                sink = jnp.zeros((SWA_GROUP * BLOCK, 1), f32)
                for g in range(SWA_GROUP):
                    sink = jnp.where(grp == g, scal_ref[hk * SWA_GROUP + g], sink)
                if streaming:
                    m = jnp.maximum(sink, bound)
                    e = jnp.where(valid4, jnp.exp2(s - m), 0.0)
                else:
                    s = jnp.where(valid4, s, -1e30)
                    m = jnp.maximum(jnp.max(s, axis=-1, keepdims=True), sink)
                    e = jnp.exp2(s - m)
                ev = jnp.dot(e.astype(bf16), v31, preferred_element_type=f32)
                o = ev[:, 0:LANES] * (1.0 / (ev[:, LANES:LANES + 1] + jnp.exp2(sink - m)))
                outs.extend(o[g * BLOCK:(g + 1) * BLOCK] for g in range(SWA_GROUP))
            for pc in range(SWA_HEADS // 2):
                hk = (2 * pc) // SWA_GROUP
                left, right = outs[2 * pc], outs[2 * pc + 1]
                if hk == 1:
                    left = pltpu.roll(left, HEAD_DIM, 1)
                else:
                    right = pltpu.roll(right, HEAD_DIM, 1)
                o_ref[rows, pc * LANES:(pc + 1) * LANES] = jnp.where(lane < HEAD_DIM, left, right).astype(bf16)

    @pl.when(bound <= EXP_SHIFT_LIMIT)
    def _():
        run(True)

    @pl.when(bound > EXP_SHIFT_LIMIT)
    def _():
        run(False)


def _band_attn(scal, sqp, sk, sv, batch, seq):
    t = sqp.shape[0]
    nb = seq // BLOCK
    nb2 = nb // 2
    prev = lambda b, m: (b * nb + jnp.maximum(2 * m - 1, 0), 0)
    pair = lambda b, m: (b * nb2 + m, 0)
    nxt = lambda b, m: (b * nb + jnp.minimum(2 * m + 2, nb - 1), 0)
    one = lambda f: pl.BlockSpec((BLOCK, LANES), f)
    two = pl.BlockSpec((2 * BLOCK, LANES), pair)
    return pl.pallas_call(
        _band_attn_kernel,
        grid=(batch, nb2),
        in_specs=[pl.BlockSpec(memory_space=pltpu.SMEM), pl.BlockSpec((2 * BLOCK, SWA_HEADS * LANES), pair),
                  one(prev), two, one(nxt), one(prev), two, one(nxt)],
        out_specs=pl.BlockSpec((2 * BLOCK, SWA_WIDTH), pair),
        out_shape=jax.ShapeDtypeStruct((t, SWA_WIDTH), bf16),
        compiler_params=_cparams(("arbitrary", "arbitrary")),
        name="band_attn",
    )(scal, sqp, sk, sk, sk, sv, sv, sv)


def _attn_out_kernel(do_ref, so_ref, x_ref, w_ref, g_ref, rw_ref, rhi_ref, h_ref, hn_ref, afft_ref):
    mix = jnp.concatenate([do_ref[...], so_ref[...]], axis=1)
    h = x_ref[...] + jnp.dot(mix, w_ref[...], preferred_element_type=f32)
    h_ref[...] = h
    ms = jnp.mean(h * h, axis=-1, keepdims=True)
    hn = h * lax.rsqrt(ms + EPS) * g_ref[...]
    hn_ref[...] = hn.reshape(hn.shape[0], ROW_TILES, LANES)
    xh = hn.astype(bf16)
    xl = (hn - xh.astype(f32)).astype(bf16)
    both = jnp.dot(xh, rw_ref[...], preferred_element_type=f32)
    logits = both[:, 0:LANES] + (both[:, LANES:2 * LANES] + jnp.dot(xl, rhi_ref[...], preferred_element_type=f32))
    lane = _lane_iota(logits.shape)
    live = lane < N_EXPERTS
    lm = jnp.where(live, logits, -1e30)
    m = jnp.max(lm, axis=-1, keepdims=True)
    e = jnp.where(live, jnp.exp(lm - m), 0.0)
    aff = e / jnp.sum(e, axis=-1, keepdims=True)
    afft_ref[...] = aff.T[0:N_EXPERTS, :]


def _attn_out(do, so, x2, w, g, rw, rhi, tb):
    t = x2.shape[0]
    row = lambda i: (i, 0)
    const = lambda i: (0, 0)
    return pl.pallas_call(
        _attn_out_kernel,
        grid=(t // tb,),
        in_specs=[pl.BlockSpec((tb, 512), row), pl.BlockSpec((tb, 512), row), pl.BlockSpec((tb, D_MODEL), row),
                  pl.BlockSpec((D_MODEL, D_MODEL), const),
                  pl.BlockSpec((1, D_MODEL), const), pl.BlockSpec((D_MODEL, 2 * LANES), const),
                  pl.BlockSpec((D_MODEL, LANES), const)],
        out_specs=[pl.BlockSpec((tb, D_MODEL), row), pl.BlockSpec((tb, ROW_TILES, LANES), lambda i: (i, 0, 0)),
                   pl.BlockSpec((N_EXPERTS, tb), lambda i: (0, i))],
        out_shape=[jax.ShapeDtypeStruct((t, D_MODEL), f32), jax.ShapeDtypeStruct((t, ROW_TILES, LANES), f32),
                   jax.ShapeDtypeStruct((N_EXPERTS, t), f32)],
        compiler_params=_cparams(("arbitrary",)),
        name="attn_out",
    )(do, so, x2, w, g, rw, rhi)


def _select_kernel(a_ref, o_ref, *, capacity):
    bits = pltpu.bitcast(a_ref[...], jnp.int32)

    def count_ge(cand):
        return jnp.sum((bits >= cand).astype(f32), axis=1, keepdims=True)

    def body(i, thr):
        cand = thr | jnp.left_shift(jnp.int32(1), 30 - i)
        return jnp.where(count_ge(cand) >= capacity, cand, thr)

    thr = lax.fori_loop(0, 31, body, jnp.zeros((N_EXPERTS, 1), jnp.int32))
    need = capacity - count_ge(thr + 1)
    lane = _lane_iota((N_EXPERTS, LANES))
    o_ref[...] = jnp.where(lane == 0, pltpu.bitcast(thr, f32), need)


def _select(afft, capacity):
    return pl.pallas_call(
        functools.partial(_select_kernel, capacity=float(capacity)),
        out_shape=jax.ShapeDtypeStruct((N_EXPERTS, LANES), f32),
        compiler_params=pltpu.CompilerParams(vmem_limit_bytes=VMEM_LIMIT),
        name="select",
    )(afft)


def _rank_kernel(afft_ref, thr_ref, need_ref, tok_ref, rank_ref, offs_out_ref, carry_ref, offs_ref):
    c = pl.program_id(0)

    @pl.when(c == 0)
    def _():
        tok_ref[...] = jnp.zeros_like(tok_ref)
        carry_ref[...] = jnp.zeros_like(carry_ref)
        for e in range(N_EXPERTS):
            offs_ref[e] = 0

    a = afft_ref[...]
    thr = thr_ref[...]
    gt = a > thr
    eq = a == thr
    ri = lax.broadcasted_iota(jnp.int32, (RANK_CHUNK, RANK_CHUNK), 0)
    ci = lax.broadcasted_iota(jnp.int32, (RANK_CHUNK, RANK_CHUNK), 1)
    before = (ri < ci).astype(bf16)
    eq_rank = jnp.dot(eq.astype(bf16), before, preferred_element_type=f32) + carry_ref[:, 0:1]
    sel = gt | (eq & (eq_rank < need_ref[...]))
    local = jnp.dot(sel.astype(bf16), before, preferred_element_type=f32)
    local = jnp.where(sel, local, NOT_MEMBER)
    rank_ref[...] = local.T
    n_sel = jnp.sum(sel.astype(f32), axis=1, keepdims=True)
    carry_ref[:, 0:1] = carry_ref[:, 0:1] + jnp.sum(eq.astype(f32), axis=1, keepdims=True)

    sub = lax.broadcasted_iota(jnp.int32, (8, RANK_CHUNK), 0)
    loc = lax.broadcasted_iota(jnp.int32, (8, RANK_CHUNK), 1)
    ids = jnp.where(sub == 0, loc, jnp.where(sub == 1, 1, 0)).astype(f32)
    base = (c * RANK_CHUNK).astype(f32)
    offs = [offs_ref[e] for e in range(N_EXPERTS)]
    counts = [jnp.sum(n_sel[e:e + 1, :]).astype(jnp.int32) for e in range(N_EXPERTS)]
    reach = jnp.int32(0)
    for e in range(N_EXPERTS):
        reach = jnp.maximum(reach, offs[e] % LANES + counts[e])

    def place(n_tiles):
        slot = lax.broadcasted_iota(jnp.int32, (n_tiles * LANES, RANK_CHUNK), 0).astype(f32)
        for e in range(N_EXPERTS):
            j0 = offs[e] // LANES
            shift = (offs[e] - j0 * LANES).astype(f32)
            row_rank = local[e:e + 1, :]
            rel = jnp.where(row_rank >= 0.0, row_rank + shift, -1.0)
            g = a[e:e + 1, :]
            g_hi = g.astype(bf16).astype(f32)
            g_mid = (g - g_hi).astype(bf16).astype(f32)
            g_lo = ((g - g_hi) - g_mid).astype(bf16).astype(f32)
            lhs = jnp.where(sub == 2, g_hi, jnp.where(sub == 3, g_mid, jnp.where(sub == 4, g_lo, ids))).astype(bf16)
            onehot = jnp.where(slot == rel, 1.0, 0.0).astype(bf16)
            placed = lax.dot_general(lhs, onehot, (((1,), (1,)), ((), ())), preferred_element_type=f32)
            row = lax.broadcasted_iota(jnp.int32, placed.shape, 0)
            filled = jnp.broadcast_to(placed[1:2, :], placed.shape)
            placed = jnp.where(row == 0, placed + base * filled, placed)
            for k in range(n_tiles):
                tok_ref[e, j0 + k] = tok_ref[e, j0 + k] + placed[:, k * LANES:(k + 1) * LANES]

    @pl.when(reach <= 2 * LANES)
    def _():
        place(2)

    @pl.when(reach > 2 * LANES)
    def _():
        place(3)

    for e in range(N_EXPERTS):
        offs_ref[e] = offs[e] + counts[e]
        offs_out_ref[c * N_EXPERTS + e] = offs[e]
        offs_out_ref[(c + 1) * N_EXPERTS + e] = offs[e] + counts[e]


def _rank(afft, thr, need, capacity):
    t = afft.shape[1]
    n_chunks = t // RANK_CHUNK
    n_tiles = capacity // LANES + 3
    const2 = lambda c: (0, 0)
    return pl.pallas_call(
        _rank_kernel,
        grid=(n_chunks,),
        in_specs=[pl.BlockSpec((N_EXPERTS, RANK_CHUNK), lambda c: (0, c)),
                  pl.BlockSpec((N_EXPERTS, 1), const2), pl.BlockSpec((N_EXPERTS, 1), const2)],
        out_specs=[pl.BlockSpec((N_EXPERTS, n_tiles, 8, LANES), lambda c: (0, 0, 0, 0)),
                   pl.BlockSpec((RANK_CHUNK, N_EXPERTS), lambda c: (c, 0)),
                   pl.BlockSpec(memory_space=pltpu.SMEM)],
        out_shape=[jax.ShapeDtypeStruct((N_EXPERTS, n_tiles, 8, LANES), f32),
                   jax.ShapeDtypeStruct((t, N_EXPERTS), f32),
                   jax.ShapeDtypeStruct(((n_chunks + 1) * N_EXPERTS,), jnp.int32)],
        scratch_shapes=[pltpu.VMEM((N_EXPERTS, LANES), f32), pltpu.SMEM((N_EXPERTS,), jnp.int32)],
        compiler_params=_cparams(("arbitrary",)),
        name="rank",
    )(afft, thr, need)


DMA_UNROLL = 8
FFN_COL_BLOCKS = 4
FFN_SLOTS = 3


def _ffn_kernel(tok_ref, nxt_ref, far_ref, lst_ref, rows_ref, wg_ref, wu_ref, wd_ref, o_ref, xbuf, sem):
    s = pl.program_id(0)
    n = pl.num_programs(0) - 1
    slot = s % FFN_SLOTS
    ahead = (s + 2) % FFN_SLOTS
    tc = xbuf.shape[1]

    def row_copy(idx_ref, k, r):
        return pltpu.make_async_copy(rows_ref.at[pl.ds(idx_ref[0, 0, r], 1)], xbuf.at[k, pl.ds(r, 1)], sem.at[k])

    def wait_rows(k):
        pltpu.make_async_copy(rows_ref.at[pl.ds(0, tc)], xbuf.at[k], sem.at[k]).wait()

    @pl.when(s == 0)
    def _():
        def body(r, carry):
            row_copy(tok_ref, 0, r).start()
            row_copy(nxt_ref, 1, r).start()
            return carry
        lax.fori_loop(0, tc, body, 0, unroll=DMA_UNROLL)

    @pl.when(s < n)
    def _():
        wait_rows(slot)
        x = xbuf[slot].reshape(tc, D_MODEL).astype(bf16)
        per = tc // (2 * FFN_COL_BLOCKS)

        def issue(group):
            for r in range(group * per, (group + 1) * per):
                row_copy(far_ref, ahead, r).start()

        cw = EXPERT_FF // FFN_COL_BLOCKS
        mids = []
        for b in range(FFN_COL_BLOCKS):
            issue(b)
            g = jnp.dot(x, wg_ref[0, :, b * cw:(b + 1) * cw], preferred_element_type=f32)
            u = jnp.dot(x, wu_ref[0, :, b * cw:(b + 1) * cw], preferred_element_type=f32)
            mids.append((g * jax.nn.sigmoid(g) * u).astype(bf16))
        hmid = jnp.concatenate(mids, axis=1)
        lst = lst_ref[0]
        gates = lst[:, 2, :] + lst[:, 3, :] + lst[:, 4, :]
        gates_t = jnp.concatenate([gates, jnp.zeros((8 - gates.shape[0], LANES), f32)], axis=0).T
        gate = jnp.concatenate([gates_t[:, k:k + 1] for k in range(gates.shape[0])], axis=0)
        ow = D_MODEL // FFN_COL_BLOCKS
        for b in range(FFN_COL_BLOCKS):
            issue(FFN_COL_BLOCKS + b)
            o_ref[:, b * ow:(b + 1) * ow] = (jnp.dot(hmid, wd_ref[0, :, b * ow:(b + 1) * ow],
                                                     preferred_element_type=f32) * gate).astype(bf16)

    @pl.when(s == n)
    def _():
        wait_rows(slot)
        wait_rows((s + 1) % FFN_SLOTS)
        o_ref[...] = jnp.zeros_like(o_ref)


def _ffn(tok, lists, rows, wg, wu, wd, capacity, tc):
    nt = capacity // tc
    lt = tc // LANES
    n = N_EXPERTS * nt
    expert = lambda s: (jnp.minimum(s // nt, N_EXPERTS - 1), 0, 0)
    return pl.pallas_call(
        _ffn_kernel,
        grid=(n + 1,),
        in_specs=[pl.BlockSpec((1, 1, tc), lambda s: (jnp.minimum(s, n - 1), 0, 0), memory_space=pltpu.SMEM),
                  pl.BlockSpec((1, 1, tc), lambda s: (jnp.minimum(s + 1, n - 1), 0, 0), memory_space=pltpu.SMEM),
                  pl.BlockSpec((1, 1, tc), lambda s: (jnp.minimum(s + 2, n - 1), 0, 0), memory_space=pltpu.SMEM),
                  pl.BlockSpec((1, lt, 8, LANES), lambda s: (jnp.minimum(s // nt, N_EXPERTS - 1), s % nt, 0, 0)),
                  pl.BlockSpec(memory_space=pl.ANY),
                  pl.BlockSpec((1, D_MODEL, EXPERT_FF), expert), pl.BlockSpec((1, D_MODEL, EXPERT_FF), expert),
                  pl.BlockSpec((1, EXPERT_FF, D_MODEL), expert)],
        out_specs=pl.BlockSpec((tc, D_MODEL), lambda s: (s, 0)),
        out_shape=jax.ShapeDtypeStruct(((n + 1) * tc, D_MODEL), bf16),
        scratch_shapes=[pltpu.VMEM((FFN_SLOTS, tc, ROW_TILES, LANES), f32), pltpu.SemaphoreType.DMA((FFN_SLOTS,))],
        compiler_params=_cparams(("arbitrary",)),
        name="ffn",
    )(tok, tok, tok, lists, rows, wg, wu, wd)


WIN = 128
ROW_ALIGN = 16


def _combine_kernel(offs_ref, h_ref, rank_ref, ye_ref, y_ref, win, sem, *, capacity):
    c = pl.program_id(0)
    n_chunks = pl.num_programs(0)
    slot = c % 2
    width = N_EXPERTS * WIN

    def window_starts(chunk):
        return [(offs_ref[chunk * N_EXPERTS + e] // ROW_ALIGN) * ROW_ALIGN for e in range(N_EXPERTS)]

    def copy(e, start, p, k):
        src = jnp.minimum(e * capacity + start + p * WIN, ye_ref.shape[0] - WIN)
        return pltpu.make_async_copy(ye_ref.at[pl.ds(pl.multiple_of(src, ROW_ALIGN), WIN)],
                                     win.at[k, pl.ds(e * WIN, WIN)], sem.at[k])

    starts = window_starts(c)
    n_pass = jnp.int32(0)
    for e in range(N_EXPERTS):
        st = offs_ref[c * N_EXPERTS + e]
        cnt = offs_ref[(c + 1) * N_EXPERTS + e] - st
        n_pass = jnp.maximum(n_pass, jnp.where(cnt > 0, (st - starts[e] + cnt + WIN - 1) // WIN, 0))

    @pl.when(c == 0)
    def _():
        for e in range(N_EXPERTS):
            copy(e, starts[e], 0, slot).start()

    @pl.when(c + 1 < n_chunks)
    def _():
        nxt = window_starts(c + 1)
        for e in range(N_EXPERTS):
            copy(e, nxt[e], 0, 1 - slot).start()

    col = lax.broadcasted_iota(jnp.int32, (N_EXPERTS, width), 1)
    row = lax.broadcasted_iota(jnp.int32, (N_EXPERTS, width), 0)
    spread = jnp.where((col // WIN) == row, 1.0, 0.0).astype(bf16)
    rank_rep = jnp.dot(rank_ref[...].astype(bf16), spread, preferred_element_type=f32)
    col1 = lax.broadcasted_iota(jnp.int32, (1, width), 1)
    base = jnp.zeros((1, width), jnp.int32)
    for e in range(N_EXPERTS):
        base = jnp.where((col1 // WIN) == e, starts[e] - offs_ref[c * N_EXPERTS + e], base)
    base = (base + col1 % WIN).astype(f32)

    def scatter_matmul(p, k):
        hit = rank_rep == base + (p * WIN).astype(f32)
        sel = jnp.where(hit, 1.0, 0.0).astype(bf16)
        return jnp.dot(sel, win[k], preferred_element_type=f32)

    def wait_windows(k):
        for e in range(N_EXPERTS):
            copy(e, starts[e], 0, k).wait()

    wait_windows(slot)
    acc = scatter_matmul(jnp.int32(0), slot)

    def later_pass(p, acc):
        for e in range(N_EXPERTS):
            copy(e, starts[e], p, slot).start()
        wait_windows(slot)
        return acc + scatter_matmul(p, slot)

    y_ref[...] = h_ref[...] + lax.fori_loop(1, n_pass, later_pass, acc)


def _combine(offs, h, rank16, ye, capacity):
    t = h.shape[0]
    row = lambda c, offs: (c, 0)
    return pl.pallas_call(
        functools.partial(_combine_kernel, capacity=capacity),
        grid_spec=pltpu.PrefetchScalarGridSpec(
            num_scalar_prefetch=1,
            grid=(t // RANK_CHUNK,),
            in_specs=[pl.BlockSpec((RANK_CHUNK, D_MODEL), row), pl.BlockSpec((RANK_CHUNK, N_EXPERTS), row),
                      pl.BlockSpec(memory_space=pl.ANY)],
            out_specs=pl.BlockSpec((RANK_CHUNK, D_MODEL), row),
            scratch_shapes=[pltpu.VMEM((2, N_EXPERTS * WIN, D_MODEL), bf16), pltpu.SemaphoreType.DMA((2,))]),
        out_shape=jax.ShapeDtypeStruct((t, D_MODEL), f32),
        compiler_params=_cparams(("arbitrary",)),
        name="combine",
    )(offs, h, rank16, ye)


def _rope_tables(seq):
    half = HEAD_DIM // 2
    inv = 1.0 / (ROPE_THETA ** (jnp.arange(0, HEAD_DIM, 2, dtype=f32) / HEAD_DIM))
    ang = jnp.arange(seq, dtype=f32)[:, None] * inv[None, :]
    cos = jnp.tile(jnp.cos(ang), (1, LANES // half))
    sign = jnp.where((jnp.arange(LANES) % HEAD_DIM) < half, -1.0, 1.0).astype(f32)
    sin = jnp.tile(jnp.sin(ang), (1, LANES // half)) * sign[None, :]
    return cos, sin


def _block_diag_mean(width):
    i = jnp.arange(width)
    return jnp.where((i[:, None] // HEAD_DIM) == (i[None, :] // HEAD_DIM), 1.0 / HEAD_DIM, 0.0).astype(bf16)


def _pick_tile(n, pref):
    while n % pref:
        pref //= 2
    return pref


def _trunk(x, p):
    batch, seq, _ = x.shape
    t = batch * seq
    capacity = CAPACITY_FACTOR * t // N_EXPERTS
    assert seq % (2 * BLOCK) == 0 and t % RANK_CHUNK == 0 and capacity % LANES == 0
    tb = _pick_tile(seq, 512)
    tq = _pick_tile(seq, 256)
    kc = _pick_tile(seq, DIFF_KEY_CHUNK)
    heads = DIFF_HEADS if seq <= 2048 else DIFF_HEADS // 2
    tc = _pick_tile(capacity, FFN_TILE)
    cos, sin = _rope_tables(seq)
    gm256, gm128 = _block_diag_mean(256), _block_diag_mean(128)
    x2 = x.reshape(t, D_MODEL)
    depth = p["w_in"].shape[0]
    for l in range(depth):
        lam_init = 0.8 - 0.6 * math.exp(-0.3 * l)
        lam = (jnp.exp(jnp.sum(p["lambda_q1"][l] * p["lambda_k1"][l]))
               - jnp.exp(jnp.sum(p["lambda_q2"][l] * p["lambda_k2"][l])) + lam_init)
        bound = (8.0 * LOG2E * (1.0 + 2.0 ** -6)) * jnp.max(jnp.abs(p["diff_q_norm"][l])) \
            * jnp.max(jnp.abs(p["diff_k_norm"][l]))
        scal = jnp.stack([lam, jnp.asarray(1.0 - lam_init, f32), bound]).astype(f32)
        tile8 = lambda v: jnp.tile(v, 8)[None, :]
        dq, dk, dv, sqp, sk, sv = _attn_in(
            x2, p["attn_norm"][l][None, :], p["w_in_bf"][l], cos, sin,
            tile8(p["diff_q_norm"][l]), tile8(p["diff_k_norm"][l]), tile8(p["swa_q_norm"][l]),
            jnp.tile(p["swa_k_norm"][l], 2)[None, :], gm256, gm128, seq, tb)
        do = _diff_attn(scal, dq, dk, dv, p["diff_subln"][l][None, :], batch, seq, tq, heads, kc)
        swa_bound = (8.0 * LOG2E * (1.0 + 2.0 ** -6)) * jnp.max(jnp.abs(p["swa_q_norm"][l])) \
            * jnp.max(jnp.abs(p["swa_k_norm"][l]))
        so = _band_attn(jnp.concatenate([p["swa_sink"][l] * LOG2E, swa_bound[None]]).astype(f32), sqp, sk, sv, batch, seq)
        h, rows, afft = _attn_out(do, so, x2, p["w_out_bf"][l], p["ffn_norm"][l][None, :],
                                   jnp.concatenate([p["r_hi"][l], p["r_lo"][l]], axis=1), p["r_hi"][l], tb)
        sel = _select(afft, capacity)
        tokbuf, rank16, offs = _rank(afft, sel[:, 0:1], sel[:, 1:2], capacity)
        tok = tokbuf[:, :capacity // LANES, 0, :].reshape(N_EXPERTS * capacity // tc, 1, tc).astype(jnp.int32)
        ye = _ffn(tok, tokbuf, rows, p["w_gate_bf"][l], p["w_up_bf"][l], p["w_down_bf"][l], capacity, tc)
        x2 = _combine(offs, h, rank16, ye, capacity)
    return x2.reshape(batch, seq, D_MODEL)


def kernel(x_prompt, x_sample, attn_norm, w_in, diff_q_norm, diff_k_norm, lambda_q1, lambda_k1, lambda_q2, lambda_k2,
           diff_subln, swa_q_norm, swa_k_norm, swa_sink, w_out, ffn_norm, w_router, w_gate, w_up, w_down):
    r_pad = jnp.pad(w_router, ((0, 0), (0, 0), (0, LANES - N_EXPERTS)))
    r_hi = r_pad.astype(bf16)
    p = dict(attn_norm=attn_norm, diff_q_norm=diff_q_norm, diff_k_norm=diff_k_norm, lambda_q1=lambda_q1,
             lambda_k1=lambda_k1, lambda_q2=lambda_q2, lambda_k2=lambda_k2, diff_subln=diff_subln,
             swa_q_norm=swa_q_norm, swa_k_norm=swa_k_norm, swa_sink=swa_sink, ffn_norm=ffn_norm,
             w_in=w_in, w_in_bf=w_in.astype(bf16), w_out_bf=w_out.astype(bf16),
             r_hi=r_hi, r_lo=(r_pad - r_hi.astype(f32)).astype(bf16),
             w_gate_bf=w_gate.astype(bf16), w_up_bf=w_up.astype(bf16), w_down_bf=w_down.astype(bf16))
    return (_trunk(x_prompt, p), _trunk(x_sample, p))
```

```python
import functools
import math

import jax
import jax.numpy as jnp
from jax import lax
from jax.experimental import pallas as pl
from jax.experimental.pallas import tpu as pltpu

D_MODEL = 1024
HEAD_DIM = 64
DIFF_HEADS = 4
DIFF_VDIM = 128
DIFF_WIDTH = 512
SWA_HEADS = 8
SWA_KV_HEADS = 2
SWA_GROUP = 4
SWA_WIDTH = 512
BLOCK = 128
N_EXPERTS = 16
EXPERT_FF = 1024
CAPACITY_FACTOR = 2
ROPE_THETA = 10000.0
EPS = 1e-6
LOG2E = 1.4426950408889634
EXP_SHIFT_LIMIT = 60.0
DIFF_KEY_CHUNK = 256
IN_WIDTH = 2304
LANES = 128
ROW_TILES = D_MODEL // LANES
RANK_CHUNK = 256
NOT_MEMBER = -256.0
FFN_TILE = 512
VMEM_LIMIT = 56 * 1024 * 1024

f32 = jnp.float32
bf16 = jnp.bfloat16


def _cparams(sem):
    return pltpu.CompilerParams(dimension_semantics=sem, vmem_limit_bytes=VMEM_LIMIT)


def _lane_iota(shape):
    return lax.broadcasted_iota(jnp.int32, shape, len(shape) - 1)


def _head_norm_rope(u, gain, gmat, cos, sin_signed):
    w = u.shape[1]
    gw = gmat.shape[0]
    u2 = (u * u).astype(bf16)
    ms = jnp.concatenate([jnp.dot(u2[:, i:i + gw], gmat, preferred_element_type=f32) for i in range(0, w, gw)], axis=1)
    un = u * lax.rsqrt(ms + EPS) * gain
    lane = _lane_iota(un.shape)
    first_half = (lane % HEAD_DIM) < (HEAD_DIM // 2)
    rot = jnp.where(first_half, pltpu.roll(un, w - HEAD_DIM // 2, 1), pltpu.roll(un, HEAD_DIM // 2, 1))
    return un * cos + rot * sin_signed


def _attn_in_kernel(x_ref, g_ref, w_ref, cos_ref, sin_ref, gdq_ref, gdk_ref, gsq_ref, gsk_ref, gm256_ref, gm128_ref,
                    dq_ref, dk_ref, dv_ref, sq_ref, sk_ref, sv_ref):
    x = x_ref[...]
    ms = jnp.mean(x * x, axis=-1, keepdims=True)
    xn = (x * lax.rsqrt(ms + EPS) * g_ref[...]).astype(bf16)
    proj = lambda lo, hi: jnp.dot(xn, w_ref[:, lo:hi], preferred_element_type=f32)
    cos1, sin1 = cos_ref[...], sin_ref[...]
    cos4 = jnp.concatenate([cos1] * 4, axis=1)
    sin4 = jnp.concatenate([sin1] * 4, axis=1)
    gm256 = gm256_ref[...]
    scale = (1.0 / math.sqrt(HEAD_DIM)) * LOG2E
    dq = _head_norm_rope(proj(0, 512), gdq_ref[...], gm256, cos4, sin4)
    dk = _head_norm_rope(proj(512, 1024), gdk_ref[...], gm256, cos4, sin4)
    sq = _head_norm_rope(proj(1536, 2048), gsq_ref[...], gm256, cos4, sin4)
    skv = proj(2048, 2304)
    sk = _head_norm_rope(skv[:, 0:128], gsk_ref[...], gm128_ref[...], cos1, sin1)
    dq_ref[...] = (dq * scale).astype(bf16)
    dk_ref[...] = dk.astype(bf16)
    dv_ref[...] = proj(1024, 1536).astype(bf16)
    sk_ref[...] = sk.astype(bf16)
    sv_ref[...] = skv[:, 128:256].astype(bf16)
    sq = sq * scale
    lane = _lane_iota((sq.shape[0], LANES))
    for j in range(SWA_HEADS):
        blk = sq[:, (j // 2) * LANES:(j // 2 + 1) * LANES]
        hk = j // SWA_GROUP
        if (j % 2) != hk:
            blk = pltpu.roll(blk, HEAD_DIM, 1)
        blk = jnp.where((lane // HEAD_DIM) == hk, blk, 0.0)
        sq_ref[:, j * LANES:(j + 1) * LANES] = blk.astype(bf16)


def _attn_in(x2, g, w_in, cos, sin, gdq, gdk, gsq, gsk, gm256, gm128, seq, tb):
    t = x2.shape[0]
    nb_seq = seq // tb
    row = lambda i: (i, 0)
    const = lambda i: (0, 0)
    outs = [jax.ShapeDtypeStruct((t, 512), bf16)] * 3 + [jax.ShapeDtypeStruct((t, 1024), bf16)] + \
           [jax.ShapeDtypeStruct((t, 128), bf16)] * 2
    return pl.pallas_call(
        _attn_in_kernel,
        grid=(t // tb,),
        in_specs=[
            pl.BlockSpec((tb, D_MODEL), row),
            pl.BlockSpec((1, D_MODEL), const),
            pl.BlockSpec((D_MODEL, IN_WIDTH), const),
            pl.BlockSpec((tb, LANES), lambda i: (i % nb_seq, 0)),
            pl.BlockSpec((tb, LANES), lambda i: (i % nb_seq, 0)),
            pl.BlockSpec((1, 512), const), pl.BlockSpec((1, 512), const), pl.BlockSpec((1, 512), const),
            pl.BlockSpec((1, 128), const),
            pl.BlockSpec((256, 256), const), pl.BlockSpec((128, 128), const),
        ],
        out_specs=[pl.BlockSpec((tb, 512), row)] * 3 + [pl.BlockSpec((tb, 1024), row)] + [pl.BlockSpec((tb, 128), row)] * 2,
        out_shape=outs,
        compiler_params=_cparams(("arbitrary",)),
        name="attn_in",
    )(x2, g, w_in, cos, sin, gdq, gdk, gsq, gsk, gm256, gm128)


def _diff_attn_kernel(scal_ref, q_ref, k_ref, v_ref, g_ref, o_ref, *, kc):
    lam = scal_ref[0]
    out_scale = scal_ref[1]
    bound = scal_ref[2]
    tq = q_ref.shape[0]
    n_heads = q_ref.shape[1] // LANES
    n_chunks = k_ref.shape[0] // kc
    lane = _lane_iota((tq, LANES))

    def stacked_q(h):
        q = q_ref[:, h * LANES:(h + 1) * LANES]
        zero = jnp.zeros_like(q)
        return jnp.concatenate([jnp.where(lane < HEAD_DIM, q, zero), jnp.where(lane >= HEAD_DIM, q, zero)], axis=0)

    def value_block(rows, h):
        vblk = v_ref[rows, h * LANES:(h + 1) * LANES]
        return jnp.concatenate([vblk, jnp.ones_like(vblk)], axis=1)

    def finish(ev, h):
        on = ev[:, 0:LANES] * (1.0 / ev[:, LANES:LANES + 1])
        o = on[:tq] - lam * on[tq:]
        ms = jnp.mean(o * o, axis=-1, keepdims=True)
        o_ref[:, h * LANES:(h + 1) * LANES] = ((o * lax.rsqrt(ms + EPS) * g_ref[...]) * out_scale).astype(bf16)

    @pl.when(bound <= EXP_SHIFT_LIMIT)
    def _():
        for h in range(n_heads):
            qs = stacked_q(h)
            ev = jnp.zeros((2 * tq, 2 * LANES), f32)
            for c in range(n_chunks):
                rows = slice(c * kc, (c + 1) * kc)
                s = lax.dot_general(qs, k_ref[rows, h * LANES:(h + 1) * LANES], (((1,), (1,)), ((), ())),
                                    preferred_element_type=f32)
                ev = ev + jnp.dot(jnp.exp2(s - bound).astype(bf16), value_block(rows, h), preferred_element_type=f32)
            finish(ev, h)

    @pl.when(bound > EXP_SHIFT_LIMIT)
    def _():
        for h in range(n_heads):
            qs = stacked_q(h)

            def chunk(c, carry, qs=qs, h=h):
                m, ev = carry
                rows = pl.ds(pl.multiple_of(c * kc, kc), kc)
                s = lax.dot_general(qs, k_ref[rows, h * LANES:(h + 1) * LANES], (((1,), (1,)), ((), ())),
                                    preferred_element_type=f32)
                m_new = jnp.maximum(m, jnp.max(s, axis=-1, keepdims=True))
                ev = jnp.exp2(m - m_new) * ev + jnp.dot(jnp.exp2(s - m_new).astype(bf16), value_block(rows, h),
                                                        preferred_element_type=f32)
                return m_new, ev

            init = (jnp.full((2 * tq, 1), -jnp.inf, f32), jnp.zeros((2 * tq, 2 * LANES), f32))
            finish(lax.fori_loop(0, n_chunks, chunk, init)[1], h)


def _diff_attn(scal, dq, dk, dv, g, batch, seq, tq, heads, kc):
    t = dq.shape[0]
    nq = seq // tq
    w = heads * LANES
    return pl.pallas_call(
        functools.partial(_diff_attn_kernel, kc=kc),
        grid=(batch, DIFF_HEADS // heads, nq),
        in_specs=[
            pl.BlockSpec(memory_space=pltpu.SMEM),
            pl.BlockSpec((tq, w), lambda b, h, i: (b * nq + i, h)),
            pl.BlockSpec((seq, w), lambda b, h, i: (b, h)),
            pl.BlockSpec((seq, w), lambda b, h, i: (b, h)),
            pl.BlockSpec((1, LANES), lambda b, h, i: (0, 0)),
        ],
        out_specs=pl.BlockSpec((tq, w), lambda b, h, i: (b * nq + i, h)),
        out_shape=jax.ShapeDtypeStruct((t, DIFF_WIDTH), bf16),
        compiler_params=_cparams(("arbitrary", "arbitrary", "arbitrary")),
        name="diff_attn",
    )(scal, dq, dk, dv, g)


def _band_attn_kernel(scal_ref, q_ref, kp_ref, kc_ref, kn_ref, vp_ref, vc_ref, vn_ref, o_ref):
    m_blk = pl.program_id(1)
    last = pl.num_programs(1) - 1
    bound = scal_ref[SWA_HEADS]
    kc, vc = kc_ref[...], vc_ref[...]
    r = lax.broadcasted_iota(jnp.int32, (BLOCK, 3 * BLOCK), 0)
    c = lax.broadcasted_iota(jnp.int32, (BLOCK, 3 * BLOCK), 1)
    band = (c >= r) & (c <= r + 2 * BLOCK)
    lane = _lane_iota((BLOCK, LANES))
    grp = lax.broadcasted_iota(jnp.int32, (SWA_GROUP * BLOCK, 1), 0) // BLOCK

    def run(streaming):
        halves = (
            (0, kp_ref[...], kc, vp_ref[...], vc, band & ((c >= BLOCK) | (m_blk > 0))),
            (1, kc, kn_ref[...], vc, vn_ref[...], band & ((c < 2 * BLOCK) | (m_blk < last))),
        )
        for half, ka, kb, va, vb, valid in halves:
            k3 = jnp.concatenate([ka, kb], axis=0)
            v3 = jnp.concatenate([va, vb], axis=0)
            v31 = jnp.concatenate([v3, jnp.ones_like(v3)], axis=1)
            valid4 = jnp.concatenate([valid] * SWA_GROUP, axis=0)
            rows = slice(half * BLOCK, (half + 1) * BLOCK)
            outs = []
            for hk in range(SWA_KV_HEADS):
                qs = jnp.concatenate([q_ref[rows, (hk * SWA_GROUP + g) * LANES:(hk * SWA_GROUP + g + 1) * LANES]
                                      for g in range(SWA_GROUP)], axis=0)
                s = lax.dot_general(qs, k3, (((1,), (1,)), ((), ())), preferred_element_type=f32)
                sink = jnp.zeros((SWA_GROUP * BLOCK, 1), f32)
                for g in range(SWA_GROUP):
                    sink = jnp.where(grp == g, scal_ref[hk * SWA_GROUP + g], sink)
                if streaming:
                    m = jnp.maximum(sink, bound)
                    e = jnp.where(valid4, jnp.exp2(s - m), 0.0)
                else:
                    s = jnp.where(valid4, s, -1e30)
                    m = jnp.maximum(jnp.max(s, axis=-1, keepdims=True), sink)
                    e = jnp.exp2(s - m)
                ev = jnp.dot(e.astype(bf16), v31, preferred_element_type=f32)
                o = ev[:, 0:LANES] * (1.0 / (ev[:, LANES:LANES + 1] + jnp.exp2(sink - m)))
                outs.extend(o[g * BLOCK:(g + 1) * BLOCK] for g in range(SWA_GROUP))
            for pc in range(SWA_HEADS // 2):
                hk = (2 * pc) // SWA_GROUP
                left, right = outs[2 * pc], outs[2 * pc + 1]
                if hk == 1:
                    left = pltpu.roll(left, HEAD_DIM, 1)
                else:
                    right = pltpu.roll(right, HEAD_DIM, 1)
                o_ref[rows, pc * LANES:(pc + 1) * LANES] = jnp.where(lane < HEAD_DIM, left, right).astype(bf16)

    @pl.when(bound <= EXP_SHIFT_LIMIT)
    def _():
        run(True)

    @pl.when(bound > EXP_SHIFT_LIMIT)
    def _():
        run(False)


def _band_attn(scal, sqp, sk, sv, batch, seq):
    t = sqp.shape[0]
    nb = seq // BLOCK
    nb2 = nb // 2
    prev = lambda b, m: (b * nb + jnp.maximum(2 * m - 1, 0), 0)
    pair = lambda b, m: (b * nb2 + m, 0)
    nxt = lambda b, m: (b * nb + jnp.minimum(2 * m + 2, nb - 1), 0)
    one = lambda f: pl.BlockSpec((BLOCK, LANES), f)
    two = pl.BlockSpec((2 * BLOCK, LANES), pair)
    return pl.pallas_call(
        _band_attn_kernel,
        grid=(batch, nb2),
        in_specs=[pl.BlockSpec(memory_space=pltpu.SMEM), pl.BlockSpec((2 * BLOCK, SWA_HEADS * LANES), pair),
                  one(prev), two, one(nxt), one(prev), two, one(nxt)],
        out_specs=pl.BlockSpec((2 * BLOCK, SWA_WIDTH), pair),
        out_shape=jax.ShapeDtypeStruct((t, SWA_WIDTH), bf16),
        compiler_params=_cparams(("arbitrary", "arbitrary")),
        name="band_attn",
    )(scal, sqp, sk, sk, sk, sv, sv, sv)


def _attn_out_kernel(do_ref, so_ref, x_ref, w_ref, g_ref, rw_ref, rhi_ref, h_ref, hn_ref, afft_ref):
    mix = jnp.concatenate([do_ref[...], so_ref[...]], axis=1)
    h = x_ref[...] + jnp.dot(mix, w_ref[...], preferred_element_type=f32)
    h_ref[...] = h
    ms = jnp.mean(h * h, axis=-1, keepdims=True)
    hn = h * lax.rsqrt(ms + EPS) * g_ref[...]
    hn_ref[...] = hn.reshape(hn.shape[0], ROW_TILES, LANES)
    xh = hn.astype(bf16)
    xl = (hn - xh.astype(f32)).astype(bf16)
    both = jnp.dot(xh, rw_ref[...], preferred_element_type=f32)
    logits = both[:, 0:LANES] + (both[:, LANES:2 * LANES] + jnp.dot(xl, rhi_ref[...], preferred_element_type=f32))
    lane = _lane_iota(logits.shape)
    live = lane < N_EXPERTS
    lm = jnp.where(live, logits, -1e30)
    m = jnp.max(lm, axis=-1, keepdims=True)
    e = jnp.where(live, jnp.exp(lm - m), 0.0)
    aff = e / jnp.sum(e, axis=-1, keepdims=True)
    afft_ref[...] = aff.T[0:N_EXPERTS, :]


def _attn_out(do, so, x2, w, g, rw, rhi, tb):
    t = x2.shape[0]
    row = lambda i: (i, 0)
    const = lambda i: (0, 0)
    return pl.pallas_call(
        _attn_out_kernel,
        grid=(t // tb,),
        in_specs=[pl.BlockSpec((tb, 512), row), pl.BlockSpec((tb, 512), row), pl.BlockSpec((tb, D_MODEL), row),
                  pl.BlockSpec((D_MODEL, D_MODEL), const),
                  pl.BlockSpec((1, D_MODEL), const), pl.BlockSpec((D_MODEL, 2 * LANES), const),
                  pl.BlockSpec((D_MODEL, LANES), const)],
        out_specs=[pl.BlockSpec((tb, D_MODEL), row), pl.BlockSpec((tb, ROW_TILES, LANES), lambda i: (i, 0, 0)),
                   pl.BlockSpec((N_EXPERTS, tb), lambda i: (0, i))],
        out_shape=[jax.ShapeDtypeStruct((t, D_MODEL), f32), jax.ShapeDtypeStruct((t, ROW_TILES, LANES), f32),
                   jax.ShapeDtypeStruct((N_EXPERTS, t), f32)],
        compiler_params=_cparams(("arbitrary",)),
        name="attn_out",
    )(do, so, x2, w, g, rw, rhi)


def _select_kernel(a_ref, o_ref, *, capacity):
    bits = pltpu.bitcast(a_ref[...], jnp.int32)

    def count_ge(cand):
        return jnp.sum((bits >= cand).astype(f32), axis=1, keepdims=True)

    def body(i, thr):
        cand = thr | jnp.left_shift(jnp.int32(1), 30 - i)
        return jnp.where(count_ge(cand) >= capacity, cand, thr)

    thr = lax.fori_loop(0, 31, body, jnp.zeros((N_EXPERTS, 1), jnp.int32))
    need = capacity - count_ge(thr + 1)
    lane = _lane_iota((N_EXPERTS, LANES))
    o_ref[...] = jnp.where(lane == 0, pltpu.bitcast(thr, f32), need)


def _select(afft, capacity):
    return pl.pallas_call(
        functools.partial(_select_kernel, capacity=float(capacity)),
        out_shape=jax.ShapeDtypeStruct((N_EXPERTS, LANES), f32),
        compiler_params=pltpu.CompilerParams(vmem_limit_bytes=VMEM_LIMIT),
        name="select",
    )(afft)


def _rank_kernel(afft_ref, thr_ref, need_ref, tok_ref, rank_ref, offs_out_ref, ids_ref, carry_ref, offs_ref):
    c = pl.program_id(0)

    @pl.when(c == 0)
    def _():
        tok_ref[...] = jnp.zeros_like(tok_ref)
        carry_ref[...] = jnp.zeros_like(carry_ref)
        for e in range(N_EXPERTS):
            offs_ref[e] = 0

    a = afft_ref[...]
    thr = thr_ref[...]
    gt = a > thr
    eq = a == thr
    ri = lax.broadcasted_iota(jnp.int32, (RANK_CHUNK, RANK_CHUNK), 0)
    ci = lax.broadcasted_iota(jnp.int32, (RANK_CHUNK, RANK_CHUNK), 1)
    before = (ri < ci).astype(bf16)
    eq_rank = jnp.dot(eq.astype(bf16), before, preferred_element_type=f32) + carry_ref[:, 0:1]
    sel = gt | (eq & (eq_rank < need_ref[...]))
    local = jnp.dot(sel.astype(bf16), before, preferred_element_type=f32)
    local = jnp.where(sel, local, NOT_MEMBER)
    rank_ref[...] = local.T
    n_sel = jnp.sum(sel.astype(f32), axis=1, keepdims=True)
    carry_ref[:, 0:1] = carry_ref[:, 0:1] + jnp.sum(eq.astype(f32), axis=1, keepdims=True)

    sub = lax.broadcasted_iota(jnp.int32, (8, RANK_CHUNK), 0)
    loc = lax.broadcasted_iota(jnp.int32, (8, RANK_CHUNK), 1)
    ids = jnp.where(sub == 0, loc, jnp.where(sub == 1, 1, 0)).astype(f32)
    base = (c * RANK_CHUNK).astype(f32)
    offs = [offs_ref[e] for e in range(N_EXPERTS)]
    counts = [jnp.sum(n_sel[e:e + 1, :]).astype(jnp.int32) for e in range(N_EXPERTS)]
    reach = jnp.int32(0)
    for e in range(N_EXPERTS):
        reach = jnp.maximum(reach, offs[e] % LANES + counts[e])

    def place(n_tiles):
        slot = lax.broadcasted_iota(jnp.int32, (n_tiles * LANES, RANK_CHUNK), 0).astype(f32)
        for e in range(N_EXPERTS):
            j0 = offs[e] // LANES
            shift = (offs[e] - j0 * LANES).astype(f32)
            row_rank = local[e:e + 1, :]
            rel = jnp.where(row_rank >= 0.0, row_rank + shift, -1.0)
            g = a[e:e + 1, :]
            g_hi = g.astype(bf16).astype(f32)
            g_mid = (g - g_hi).astype(bf16).astype(f32)
            g_lo = ((g - g_hi) - g_mid).astype(bf16).astype(f32)
            lhs = jnp.where(sub == 2, g_hi, jnp.where(sub == 3, g_mid, jnp.where(sub == 4, g_lo, ids))).astype(bf16)
            onehot = jnp.where(slot == rel, 1.0, 0.0).astype(bf16)
            placed = lax.dot_general(lhs, onehot, (((1,), (1,)), ((), ())), preferred_element_type=f32)
            row = lax.broadcasted_iota(jnp.int32, placed.shape, 0)
            filled = jnp.broadcast_to(placed[1:2, :], placed.shape)
            placed = jnp.where(row == 0, placed + base * filled, placed)
            for k in range(n_tiles):
                tok_ref[e, j0 + k] = tok_ref[e, j0 + k] + placed[:, k * LANES:(k + 1) * LANES]

    @pl.when(reach <= 2 * LANES)
    def _():
        place(2)

    @pl.when(reach > 2 * LANES)
    def _():
        place(3)

    for e in range(N_EXPERTS):
        offs_ref[e] = offs[e] + counts[e]
        offs_out_ref[c * N_EXPERTS + e] = offs[e]
        offs_out_ref[(c + 1) * N_EXPERTS + e] = offs[e] + counts[e]

    @pl.when(c == pl.num_programs(0) - 1)
    def _():
        ids_ref[...] = tok_ref[:, :, 0, :].astype(jnp.int32)


def _rank(afft, thr, need, capacity):
    t = afft.shape[1]
    n_chunks = t // RANK_CHUNK
    n_tiles = capacity // LANES + 3
    const2 = lambda c: (0, 0)
    return pl.pallas_call(
        _rank_kernel,
        grid=(n_chunks,),
        in_specs=[pl.BlockSpec((N_EXPERTS, RANK_CHUNK), lambda c: (0, c)),
                  pl.BlockSpec((N_EXPERTS, 1), const2), pl.BlockSpec((N_EXPERTS, 1), const2)],
        out_specs=[pl.BlockSpec((N_EXPERTS, n_tiles, 8, LANES), lambda c: (0, 0, 0, 0)),
                   pl.BlockSpec((RANK_CHUNK, N_EXPERTS), lambda c: (c, 0)),
                   pl.BlockSpec(memory_space=pltpu.SMEM),
                   pl.BlockSpec((N_EXPERTS, n_tiles, LANES), lambda c: (0, 0, 0))],
        out_shape=[jax.ShapeDtypeStruct((N_EXPERTS, n_tiles, 8, LANES), f32),
                   jax.ShapeDtypeStruct((t, N_EXPERTS), f32),
                   jax.ShapeDtypeStruct(((n_chunks + 1) * N_EXPERTS,), jnp.int32),
                   jax.ShapeDtypeStruct((N_EXPERTS, n_tiles, LANES), jnp.int32)],
        scratch_shapes=[pltpu.VMEM((N_EXPERTS, LANES), f32), pltpu.SMEM((N_EXPERTS,), jnp.int32)],
        compiler_params=_cparams(("arbitrary",)),
        name="rank",
    )(afft, thr, need)


DMA_UNROLL = 8
FFN_COL_BLOCKS = 4
FFN_SLOTS = 3


def _ffn_kernel(tok_ref, nxt_ref, far_ref, lst_ref, rows_ref, wg_ref, wu_ref, wd_ref, o_ref, xbuf, sem):
    s = pl.program_id(0)
    n = pl.num_programs(0) - 1
    slot = s % FFN_SLOTS
    ahead = (s + 2) % FFN_SLOTS
    tc = xbuf.shape[1]

    def row_copy(idx_ref, k, r):
        return pltpu.make_async_copy(rows_ref.at[pl.ds(idx_ref[0, 0, r], 1)], xbuf.at[k, pl.ds(r, 1)], sem.at[k])

    def wait_rows(k):
        pltpu.make_async_copy(rows_ref.at[pl.ds(0, tc)], xbuf.at[k], sem.at[k]).wait()

    @pl.when(s == 0)
    def _():
        def body(r, carry):
            row_copy(tok_ref, 0, r).start()
            row_copy(nxt_ref, 1, r).start()
            return carry
        lax.fori_loop(0, tc, body, 0, unroll=DMA_UNROLL)

    @pl.when(s < n)
    def _():
        wait_rows(slot)
        x = xbuf[slot].reshape(tc, D_MODEL).astype(bf16)
        per = tc // (2 * FFN_COL_BLOCKS)

        def issue(group):
            for r in range(group * per, (group + 1) * per):
                row_copy(far_ref, ahead, r).start()

        cw = EXPERT_FF // FFN_COL_BLOCKS
        mids = []
        for b in range(FFN_COL_BLOCKS):
            issue(b)
            g = jnp.dot(x, wg_ref[0, :, b * cw:(b + 1) * cw], preferred_element_type=f32)
            u = jnp.dot(x, wu_ref[0, :, b * cw:(b + 1) * cw], preferred_element_type=f32)
            mids.append((g * jax.nn.sigmoid(g) * u).astype(bf16))
        hmid = jnp.concatenate(mids, axis=1)
        lst = lst_ref[0]
        gates = lst[:, 2, :] + lst[:, 3, :] + lst[:, 4, :]
        gates_t = jnp.concatenate([gates, jnp.zeros((8 - gates.shape[0], LANES), f32)], axis=0).T
        gate = jnp.concatenate([gates_t[:, k:k + 1] for k in range(gates.shape[0])], axis=0)
        ow = D_MODEL // FFN_COL_BLOCKS
        for b in range(FFN_COL_BLOCKS):
            issue(FFN_COL_BLOCKS + b)
            o_ref[:, b * ow:(b + 1) * ow] = (jnp.dot(hmid, wd_ref[0, :, b * ow:(b + 1) * ow],
                                                     preferred_element_type=f32) * gate).astype(bf16)

    @pl.when(s == n)
    def _():
        wait_rows(slot)
        wait_rows((s + 1) % FFN_SLOTS)
        o_ref[...] = jnp.zeros_like(o_ref)


def _ffn(tok, lists, rows, wg, wu, wd, capacity, tc):
    nt = capacity // tc
    lt = tc // LANES
    n = N_EXPERTS * nt
    expert = lambda s: (jnp.minimum(s // nt, N_EXPERTS - 1), 0, 0)
    return pl.pallas_call(
        _ffn_kernel,
        grid=(n + 1,),
        in_specs=[pl.BlockSpec((1, 1, tc), lambda s: (jnp.minimum(s, n - 1), 0, 0), memory_space=pltpu.SMEM),
                  pl.BlockSpec((1, 1, tc), lambda s: (jnp.minimum(s + 1, n - 1), 0, 0), memory_space=pltpu.SMEM),
                  pl.BlockSpec((1, 1, tc), lambda s: (jnp.minimum(s + 2, n - 1), 0, 0), memory_space=pltpu.SMEM),
                  pl.BlockSpec((1, lt, 8, LANES), lambda s: (jnp.minimum(s // nt, N_EXPERTS - 1), s % nt, 0, 0)),
                  pl.BlockSpec(memory_space=pl.ANY),
                  pl.BlockSpec((1, D_MODEL, EXPERT_FF), expert), pl.BlockSpec((1, D_MODEL, EXPERT_FF), expert),
                  pl.BlockSpec((1, EXPERT_FF, D_MODEL), expert)],
        out_specs=pl.BlockSpec((tc, D_MODEL), lambda s: (s, 0)),
        out_shape=jax.ShapeDtypeStruct(((n + 1) * tc, D_MODEL), bf16),
        scratch_shapes=[pltpu.VMEM((FFN_SLOTS, tc, ROW_TILES, LANES), f32), pltpu.SemaphoreType.DMA((FFN_SLOTS,))],
        compiler_params=_cparams(("arbitrary",)),
        name="ffn",
    )(tok, tok, tok, lists, rows, wg, wu, wd)


WIN = 128
WIN_SMALL = 64
ROW_ALIGN = 16


def _combine_kernel(offs_ref, h_ref, rank_ref, ye_ref, y_ref, win, sem, *, capacity):
    c = pl.program_id(0)
    n_chunks = pl.num_programs(0)
    slot = c % 2

    def window_starts(chunk):
        return [(offs_ref[chunk * N_EXPERTS + e] // ROW_ALIGN) * ROW_ALIGN for e in range(N_EXPERTS)]

    def reach(chunk, starts):
        far = jnp.int32(0)
        for e in range(N_EXPERTS):
            st = offs_ref[chunk * N_EXPERTS + e]
            cnt = offs_ref[(chunk + 1) * N_EXPERTS + e] - st
            far = jnp.maximum(far, jnp.where(cnt > 0, st - starts[e] + cnt, 0))
        return far

    def copy(e, start, p, k, w):
        src = jnp.minimum(e * capacity + start + p * w, ye_ref.shape[0] - w)
        return pltpu.make_async_copy(ye_ref.at[pl.ds(pl.multiple_of(src, ROW_ALIGN), w)],
                                     win.at[k, pl.ds(e * w, w)], sem.at[k])

    def fetch_first(chunk, k):
        st = window_starts(chunk)
        far = reach(chunk, st)
        for w, cond in ((WIN_SMALL, far <= WIN_SMALL), (WIN, far > WIN_SMALL)):
            @pl.when(cond)
            def _(w=w):
                for e in range(N_EXPERTS):
                    copy(e, st[e], 0, k, w).start()

    @pl.when(c == 0)
    def _():
        fetch_first(c, slot)

    @pl.when(c + 1 < n_chunks)
    def _():
        fetch_first(c + 1, 1 - slot)

    starts = window_starts(c)
    far = reach(c, starts)

    def scatter(w):
        width = N_EXPERTS * w
        col = lax.broadcasted_iota(jnp.int32, (N_EXPERTS, width), 1)
        row = lax.broadcasted_iota(jnp.int32, (N_EXPERTS, width), 0)
        spread = jnp.where((col // w) == row, 1.0, 0.0).astype(bf16)
        rank_rep = jnp.dot(rank_ref[...].astype(bf16), spread, preferred_element_type=f32)
        col1 = lax.broadcasted_iota(jnp.int32, (1, width), 1)
        base = jnp.zeros((1, width), jnp.int32)
        for e in range(N_EXPERTS):
            base = jnp.where((col1 // w) == e, starts[e] - offs_ref[c * N_EXPERTS + e], base)
        base = (base + col1 % w).astype(f32)

        def scatter_matmul(p):
            hit = rank_rep == base + (p * w).astype(f32)
            sel = jnp.where(hit, 1.0, 0.0).astype(bf16)
            return jnp.dot(sel, win[slot, 0:width, :], preferred_element_type=f32)

        def wait_windows():
            for e in range(N_EXPERTS):
                copy(e, starts[e], 0, slot, w).wait()

        wait_windows()
        acc = scatter_matmul(jnp.int32(0))

        def later_pass(p, acc):
            for e in range(N_EXPERTS):
                copy(e, starts[e], p, slot, w).start()
            wait_windows()
            return acc + scatter_matmul(p)

        y_ref[...] = h_ref[...] + lax.fori_loop(1, (far + w - 1) // w, later_pass, acc)

    @pl.when(far <= WIN_SMALL)
    def _():
        scatter(WIN_SMALL)

    @pl.when(far > WIN_SMALL)
    def _():
        scatter(WIN)


def _combine(offs, h, rank16, ye, capacity):
    t = h.shape[0]
    row = lambda c, offs: (c, 0)
    return pl.pallas_call(
        functools.partial(_combine_kernel, capacity=capacity),
        grid_spec=pltpu.PrefetchScalarGridSpec(
            num_scalar_prefetch=1,
            grid=(t // RANK_CHUNK,),
            in_specs=[pl.BlockSpec((RANK_CHUNK, D_MODEL), row), pl.BlockSpec((RANK_CHUNK, N_EXPERTS), row),
                      pl.BlockSpec(memory_space=pl.ANY)],
            out_specs=pl.BlockSpec((RANK_CHUNK, D_MODEL), row),
            scratch_shapes=[pltpu.VMEM((2, N_EXPERTS * WIN, D_MODEL), bf16), pltpu.SemaphoreType.DMA((2,))]),
        out_shape=jax.ShapeDtypeStruct((t, D_MODEL), f32),
        compiler_params=_cparams(("arbitrary",)),
        name="combine",
    )(offs, h, rank16, ye)


def _rope_tables(seq):
    half = HEAD_DIM // 2
    inv = 1.0 / (ROPE_THETA ** (jnp.arange(0, HEAD_DIM, 2, dtype=f32) / HEAD_DIM))
    ang = jnp.arange(seq, dtype=f32)[:, None] * inv[None, :]
    cos = jnp.tile(jnp.cos(ang), (1, LANES // half))
    sign = jnp.where((jnp.arange(LANES) % HEAD_DIM) < half, -1.0, 1.0).astype(f32)
    sin = jnp.tile(jnp.sin(ang), (1, LANES // half)) * sign[None, :]
    return cos, sin


def _block_diag_mean(width):
    i = jnp.arange(width)
    return jnp.where((i[:, None] // HEAD_DIM) == (i[None, :] // HEAD_DIM), 1.0 / HEAD_DIM, 0.0).astype(bf16)


def _pick_tile(n, pref):
    while n % pref:
        pref //= 2
    return pref


def _trunk(x, p):
    batch, seq, _ = x.shape
    t = batch * seq
    capacity = CAPACITY_FACTOR * t // N_EXPERTS
    assert seq % (2 * BLOCK) == 0 and t % RANK_CHUNK == 0 and capacity % LANES == 0
    tb = _pick_tile(seq, 512)
    tq = _pick_tile(seq, 256)
    kc = _pick_tile(seq, DIFF_KEY_CHUNK)
    heads = DIFF_HEADS
    tc = _pick_tile(capacity, FFN_TILE)
    cos, sin = _rope_tables(seq)
    gm256, gm128 = _block_diag_mean(256), _block_diag_mean(128)
    x2 = x.reshape(t, D_MODEL)
    depth = p["w_in"].shape[0]
    for l in range(depth):
        lam_init = 0.8 - 0.6 * math.exp(-0.3 * l)
        lam = (jnp.exp(jnp.sum(p["lambda_q1"][l] * p["lambda_k1"][l]))
               - jnp.exp(jnp.sum(p["lambda_q2"][l] * p["lambda_k2"][l])) + lam_init)
        bound = (8.0 * LOG2E * (1.0 + 2.0 ** -6)) * jnp.max(jnp.abs(p["diff_q_norm"][l])) \
            * jnp.max(jnp.abs(p["diff_k_norm"][l]))
        scal = jnp.stack([lam, jnp.asarray(1.0 - lam_init, f32), bound]).astype(f32)
        tile8 = lambda v: jnp.tile(v, 8)[None, :]
        dq, dk, dv, sqp, sk, sv = _attn_in(
            x2, p["attn_norm"][l][None, :], p["w_in_bf"][l], cos, sin,
            tile8(p["diff_q_norm"][l]), tile8(p["diff_k_norm"][l]), tile8(p["swa_q_norm"][l]),
            jnp.tile(p["swa_k_norm"][l], 2)[None, :], gm256, gm128, seq, tb)
        do = _diff_attn(scal, dq, dk, dv, p["diff_subln"][l][None, :], batch, seq, tq, heads, kc)
        swa_bound = (8.0 * LOG2E * (1.0 + 2.0 ** -6)) * jnp.max(jnp.abs(p["swa_q_norm"][l])) \
            * jnp.max(jnp.abs(p["swa_k_norm"][l]))
        so = _band_attn(jnp.concatenate([p["swa_sink"][l] * LOG2E, swa_bound[None]]).astype(f32), sqp, sk, sv, batch, seq)
        h, rows, afft = _attn_out(do, so, x2, p["w_out_bf"][l], p["ffn_norm"][l][None, :],
                                   jnp.concatenate([p["r_hi"][l], p["r_lo"][l]], axis=1), p["r_hi"][l], tb)
        sel = _select(afft, capacity)
        tokbuf, rank16, offs, ids = _rank(afft, sel[:, 0:1], sel[:, 1:2], capacity)
        tok = ids[:, :capacity // LANES].reshape(N_EXPERTS * capacity // tc, 1, tc)
        ye = _ffn(tok, tokbuf, rows, p["w_gate_bf"][l], p["w_up_bf"][l], p["w_down_bf"][l], capacity, tc)
        x2 = _combine(offs, h, rank16, ye, capacity)
    return x2.reshape(batch, seq, D_MODEL)


def kernel(x_prompt, x_sample, attn_norm, w_in, diff_q_norm, diff_k_norm, lambda_q1, lambda_k1, lambda_q2, lambda_k2,
           diff_subln, swa_q_norm, swa_k_norm, swa_sink, w_out, ffn_norm, w_router, w_gate, w_up, w_down):
    r_pad = jnp.pad(w_router, ((0, 0), (0, 0), (0, LANES - N_EXPERTS)))
    r_hi = r_pad.astype(bf16)
    p = dict(attn_norm=attn_norm, diff_q_norm=diff_q_norm, diff_k_norm=diff_k_norm, lambda_q1=lambda_q1,
             lambda_k1=lambda_k1, lambda_q2=lambda_q2, lambda_k2=lambda_k2, diff_subln=diff_subln,
             swa_q_norm=swa_q_norm, swa_k_norm=swa_k_norm, swa_sink=swa_sink, ffn_norm=ffn_norm,
             w_in=w_in, w_in_bf=w_in.astype(bf16), w_out_bf=w_out.astype(bf16),
             r_hi=r_hi, r_lo=(r_pad - r_hi.astype(f32)).astype(bf16),
             w_gate_bf=w_gate.astype(bf16), w_up_bf=w_up.astype(bf16), w_down_bf=w_down.astype(bf16))
    return (_trunk(x_prompt, p), _trunk(x_sample, p))
```

```python
import functools
import math

import jax
import jax.numpy as jnp
from jax import lax
from jax.experimental import pallas as pl
from jax.experimental.pallas import tpu as pltpu

D_MODEL = 1024
HEAD_DIM = 64
DIFF_HEADS = 4
DIFF_VDIM = 128
DIFF_WIDTH = 512
SWA_HEADS = 8
SWA_KV_HEADS = 2
SWA_GROUP = 4
SWA_WIDTH = 512
BLOCK = 128
N_EXPERTS = 16
EXPERT_FF = 1024
CAPACITY_FACTOR = 2
ROPE_THETA = 10000.0
EPS = 1e-6
LOG2E = 1.4426950408889634
EXP_SHIFT_LIMIT = 60.0
DIFF_KEY_CHUNK = 256
IN_WIDTH = 2304
LANES = 128
ROW_TILES = D_MODEL // LANES
RANK_CHUNK = 256
NOT_MEMBER = -256.0
FFN_TILE = 512
VMEM_LIMIT = 56 * 1024 * 1024

f32 = jnp.float32
bf16 = jnp.bfloat16


def _cparams(sem):
    return pltpu.CompilerParams(dimension_semantics=sem, vmem_limit_bytes=VMEM_LIMIT)


def _lane_iota(shape):
    return lax.broadcasted_iota(jnp.int32, shape, len(shape) - 1)


def _head_norm_rope(u, gain, gmat, cos, sin_signed):
    w = u.shape[1]
    gw = gmat.shape[0]
    u2 = (u * u).astype(bf16)
    ms = jnp.concatenate([jnp.dot(u2[:, i:i + gw], gmat, preferred_element_type=f32) for i in range(0, w, gw)], axis=1)
    un = u * lax.rsqrt(ms + EPS) * gain
    lane = _lane_iota(un.shape)
    first_half = (lane % HEAD_DIM) < (HEAD_DIM // 2)
    rot = jnp.where(first_half, pltpu.roll(un, w - HEAD_DIM // 2, 1), pltpu.roll(un, HEAD_DIM // 2, 1))
    return un * cos + rot * sin_signed


def _attn_in_kernel(x_ref, g_ref, w_ref, cos_ref, sin_ref, gdq_ref, gdk_ref, gsq_ref, gsk_ref, gm256_ref, gm128_ref,
                    dq_ref, dk_ref, dv_ref, sq_ref, sk_ref, sv_ref):
    x = x_ref[...]
    ms = jnp.mean(x * x, axis=-1, keepdims=True)
    xn = (x * lax.rsqrt(ms + EPS) * g_ref[...]).astype(bf16)
    proj = lambda lo, hi: jnp.dot(xn, w_ref[:, lo:hi], preferred_element_type=f32)
    cos1, sin1 = cos_ref[...], sin_ref[...]
    cos4 = jnp.concatenate([cos1] * 4, axis=1)
    sin4 = jnp.concatenate([sin1] * 4, axis=1)
    gm256 = gm256_ref[...]
    scale = (1.0 / math.sqrt(HEAD_DIM)) * LOG2E
    dq = _head_norm_rope(proj(0, 512), gdq_ref[...], gm256, cos4, sin4)
    dk = _head_norm_rope(proj(512, 1024), gdk_ref[...], gm256, cos4, sin4)
    sq = _head_norm_rope(proj(1536, 2048), gsq_ref[...], gm256, cos4, sin4)
    skv = proj(2048, 2304)
    sk = _head_norm_rope(skv[:, 0:128], gsk_ref[...], gm128_ref[...], cos1, sin1)
    dq_ref[...] = (dq * scale).astype(bf16)
    dk_ref[...] = dk.astype(bf16)
    dv_ref[...] = proj(1024, 1536).astype(bf16)
    sk_ref[...] = sk.astype(bf16)
    sv_ref[...] = skv[:, 128:256].astype(bf16)
    sq = sq * scale
    lane = _lane_iota((sq.shape[0], LANES))
    for j in range(SWA_HEADS):
        blk = sq[:, (j // 2) * LANES:(j // 2 + 1) * LANES]
        hk = j // SWA_GROUP
        if (j % 2) != hk:
            blk = pltpu.roll(blk, HEAD_DIM, 1)
        blk = jnp.where((lane // HEAD_DIM) == hk, blk, 0.0)
        sq_ref[:, j * LANES:(j + 1) * LANES] = blk.astype(bf16)


def _attn_in(x2, g, w_in, cos, sin, gdq, gdk, gsq, gsk, gm256, gm128, seq, tb):
    t = x2.shape[0]
    nb_seq = seq // tb
    row = lambda i: (i, 0)
    const = lambda i: (0, 0)
    outs = [jax.ShapeDtypeStruct((t, 512), bf16)] * 3 + [jax.ShapeDtypeStruct((t, 1024), bf16)] + \
           [jax.ShapeDtypeStruct((t, 128), bf16)] * 2
    return pl.pallas_call(
        _attn_in_kernel,
        grid=(t // tb,),
        in_specs=[
            pl.BlockSpec((tb, D_MODEL), row),
            pl.BlockSpec((1, D_MODEL), const),
            pl.BlockSpec((D_MODEL, IN_WIDTH), const),
            pl.BlockSpec((tb, LANES), lambda i: (i % nb_seq, 0)),
            pl.BlockSpec((tb, LANES), lambda i: (i % nb_seq, 0)),
            pl.BlockSpec((1, 512), const), pl.BlockSpec((1, 512), const), pl.BlockSpec((1, 512), const),
            pl.BlockSpec((1, 128), const),
            pl.BlockSpec((256, 256), const), pl.BlockSpec((128, 128), const),
        ],
        out_specs=[pl.BlockSpec((tb, 512), row)] * 3 + [pl.BlockSpec((tb, 1024), row)] + [pl.BlockSpec((tb, 128), row)] * 2,
        out_shape=outs,
        compiler_params=_cparams(("arbitrary",)),
        name="attn_in",
    )(x2, g, w_in, cos, sin, gdq, gdk, gsq, gsk, gm256, gm128)


def _diff_attn_kernel(scal_ref, q_ref, k_ref, v_ref, g_ref, o_ref, *, kc):
    lam = scal_ref[0]
    out_scale = scal_ref[1]
    bound = scal_ref[2]
    tq = q_ref.shape[0]
    n_heads = q_ref.shape[1] // LANES
    n_chunks = k_ref.shape[0] // kc
    lane = _lane_iota((tq, LANES))

    def stacked_q(h):
        q = q_ref[:, h * LANES:(h + 1) * LANES]
        zero = jnp.zeros_like(q)
        return jnp.concatenate([jnp.where(lane < HEAD_DIM, q, zero), jnp.where(lane >= HEAD_DIM, q, zero)], axis=0)

    def value_block(rows, h):
        vblk = v_ref[rows, h * LANES:(h + 1) * LANES]
        return jnp.concatenate([vblk, jnp.ones_like(vblk)], axis=1)

    def finish(ev, h):
        on = ev[:, 0:LANES] * (1.0 / ev[:, LANES:LANES + 1])
        o = on[:tq] - lam * on[tq:]
        ms = jnp.mean(o * o, axis=-1, keepdims=True)
        o_ref[:, h * LANES:(h + 1) * LANES] = ((o * lax.rsqrt(ms + EPS) * g_ref[...]) * out_scale).astype(bf16)

    @pl.when(bound <= EXP_SHIFT_LIMIT)
    def _():
        for h in range(n_heads):
            qs = stacked_q(h)
            ev = jnp.zeros((2 * tq, 2 * LANES), f32)
            for c in range(n_chunks):
                rows = slice(c * kc, (c + 1) * kc)
                s = lax.dot_general(qs, k_ref[rows, h * LANES:(h + 1) * LANES], (((1,), (1,)), ((), ())),
                                    preferred_element_type=f32)
                ev = ev + jnp.dot(jnp.exp2(s - bound).astype(bf16), value_block(rows, h), preferred_element_type=f32)
            finish(ev, h)

    @pl.when(bound > EXP_SHIFT_LIMIT)
    def _():
        for h in range(n_heads):
            qs = stacked_q(h)

            def chunk(c, carry, qs=qs, h=h):
                m, ev = carry
                rows = pl.ds(pl.multiple_of(c * kc, kc), kc)
                s = lax.dot_general(qs, k_ref[rows, h * LANES:(h + 1) * LANES], (((1,), (1,)), ((), ())),
                                    preferred_element_type=f32)
                m_new = jnp.maximum(m, jnp.max(s, axis=-1, keepdims=True))
                ev = jnp.exp2(m - m_new) * ev + jnp.dot(jnp.exp2(s - m_new).astype(bf16), value_block(rows, h),
                                                        preferred_element_type=f32)
                return m_new, ev

            init = (jnp.full((2 * tq, 1), -jnp.inf, f32), jnp.zeros((2 * tq, 2 * LANES), f32))
            finish(lax.fori_loop(0, n_chunks, chunk, init)[1], h)


def _diff_attn(scal, dq, dk, dv, g, batch, seq, tq, heads, kc):
    t = dq.shape[0]
    nq = seq // tq
    w = heads * LANES
    return pl.pallas_call(
        functools.partial(_diff_attn_kernel, kc=kc),
        grid=(batch, DIFF_HEADS // heads, nq),
        in_specs=[
            pl.BlockSpec(memory_space=pltpu.SMEM),
            pl.BlockSpec((tq, w), lambda b, h, i: (b * nq + i, h)),
            pl.BlockSpec((seq, w), lambda b, h, i: (b, h)),
            pl.BlockSpec((seq, w), lambda b, h, i: (b, h)),
            pl.BlockSpec((1, LANES), lambda b, h, i: (0, 0)),
        ],
        out_specs=pl.BlockSpec((tq, w), lambda b, h, i: (b * nq + i, h)),
        out_shape=jax.ShapeDtypeStruct((t, DIFF_WIDTH), bf16),
        compiler_params=_cparams(("arbitrary", "arbitrary", "arbitrary")),
        name="diff_attn",
    )(scal, dq, dk, dv, g)


def _band_attn_kernel(scal_ref, q_ref, kp_ref, kc_ref, kn_ref, vp_ref, vc_ref, vn_ref, o_ref):
    m_blk = pl.program_id(1)
    last = pl.num_programs(1) - 1
    bound = scal_ref[SWA_HEADS]
    kc, vc = kc_ref[...], vc_ref[...]
    r = lax.broadcasted_iota(jnp.int32, (BLOCK, 3 * BLOCK), 0)
    c = lax.broadcasted_iota(jnp.int32, (BLOCK, 3 * BLOCK), 1)
    band = (c >= r) & (c <= r + 2 * BLOCK)
    lane = _lane_iota((BLOCK, LANES))
    grp = lax.broadcasted_iota(jnp.int32, (SWA_GROUP * BLOCK, 1), 0) // BLOCK

    def run(streaming):
        halves = (
            (0, kp_ref[...], kc, vp_ref[...], vc, band & ((c >= BLOCK) | (m_blk > 0))),
            (1, kc, kn_ref[...], vc, vn_ref[...], band & ((c < 2 * BLOCK) | (m_blk < last))),
        )
        for half, ka, kb, va, vb, valid in halves:
            k3 = jnp.concatenate([ka, kb], axis=0)
            v3 = jnp.concatenate([va, vb], axis=0)
            v31 = jnp.concatenate([v3, jnp.ones_like(v3)], axis=1)
            valid4 = jnp.concatenate([valid] * SWA_GROUP, axis=0)
            rows = slice(half * BLOCK, (half + 1) * BLOCK)
            outs = []
            for hk in range(SWA_KV_HEADS):
                qs = jnp.concatenate([q_ref[rows, (hk * SWA_GROUP + g) * LANES:(hk * SWA_GROUP + g + 1) * LANES]
                                      for g in range(SWA_GROUP)], axis=0)
                s = lax.dot_general(qs, k3, (((1,), (1,)), ((), ())), preferred_element_type=f32)
                sink = jnp.zeros((SWA_GROUP * BLOCK, 1), f32)
                for g in range(SWA_GROUP):
                    sink = jnp.where(grp == g, scal_ref[hk * SWA_GROUP + g], sink)
                if streaming:
                    m = jnp.maximum(sink, bound)
                    e = jnp.where(valid4, jnp.exp2(s - m), 0.0)
                else:
                    s = jnp.where(valid4, s, -1e30)
                    m = jnp.maximum(jnp.max(s, axis=-1, keepdims=True), sink)
                    e = jnp.exp2(s - m)
                ev = jnp.dot(e.astype(bf16), v31, preferred_element_type=f32)
                o = ev[:, 0:LANES] * (1.0 / (ev[:, LANES:LANES + 1] + jnp.exp2(sink - m)))
                outs.extend(o[g * BLOCK:(g + 1) * BLOCK] for g in range(SWA_GROUP))
            for pc in range(SWA_HEADS // 2):
                hk = (2 * pc) // SWA_GROUP
                left, right = outs[2 * pc], outs[2 * pc + 1]
                if hk == 1:
                    left = pltpu.roll(left, HEAD_DIM, 1)
                else:
                    right = pltpu.roll(right, HEAD_DIM, 1)
                o_ref[rows, pc * LANES:(pc + 1) * LANES] = jnp.where(lane < HEAD_DIM, left, right).astype(bf16)

    @pl.when(bound <= EXP_SHIFT_LIMIT)
    def _():
        run(True)

    @pl.when(bound > EXP_SHIFT_LIMIT)
    def _():
        run(False)


def _band_attn(scal, sqp, sk, sv, batch, seq):
    t = sqp.shape[0]
    nb = seq // BLOCK
    nb2 = nb // 2
    prev = lambda b, m: (b * nb + jnp.maximum(2 * m - 1, 0), 0)
    pair = lambda b, m: (b * nb2 + m, 0)
    nxt = lambda b, m: (b * nb + jnp.minimum(2 * m + 2, nb - 1), 0)
    one = lambda f: pl.BlockSpec((BLOCK, LANES), f)
    two = pl.BlockSpec((2 * BLOCK, LANES), pair)
    return pl.pallas_call(
        _band_attn_kernel,
        grid=(batch, nb2),
        in_specs=[pl.BlockSpec(memory_space=pltpu.SMEM), pl.BlockSpec((2 * BLOCK, SWA_HEADS * LANES), pair),
                  one(prev), two, one(nxt), one(prev), two, one(nxt)],
        out_specs=pl.BlockSpec((2 * BLOCK, SWA_WIDTH), pair),
        out_shape=jax.ShapeDtypeStruct((t, SWA_WIDTH), bf16),
        compiler_params=_cparams(("arbitrary", "arbitrary")),
        name="band_attn",
    )(scal, sqp, sk, sk, sk, sv, sv, sv)


def _attn_out_kernel(do_ref, so_ref, x_ref, w_ref, g_ref, rw_ref, rhi_ref, h_ref, hn_ref, afft_ref):
    mix = jnp.concatenate([do_ref[...], so_ref[...]], axis=1)
    h = x_ref[...] + jnp.dot(mix, w_ref[...], preferred_element_type=f32)
    h_ref[...] = h
    ms = jnp.mean(h * h, axis=-1, keepdims=True)
    hn = h * lax.rsqrt(ms + EPS) * g_ref[...]
    hn_ref[...] = hn.reshape(hn.shape[0], ROW_TILES, LANES)
    xh = hn.astype(bf16)
    xl = (hn - xh.astype(f32)).astype(bf16)
    both = jnp.dot(xh, rw_ref[...], preferred_element_type=f32)
    logits = both[:, 0:LANES] + (both[:, LANES:2 * LANES] + jnp.dot(xl, rhi_ref[...], preferred_element_type=f32))
    lane = _lane_iota(logits.shape)
    live = lane < N_EXPERTS
    lm = jnp.where(live, logits, -1e30)
    m = jnp.max(lm, axis=-1, keepdims=True)
    e = jnp.where(live, jnp.exp(lm - m), 0.0)
    aff = e / jnp.sum(e, axis=-1, keepdims=True)
    afft_ref[...] = aff.T[0:N_EXPERTS, :]


def _attn_out(do, so, x2, w, g, rw, rhi, tb):
    t = x2.shape[0]
    row = lambda i: (i, 0)
    const = lambda i: (0, 0)
    return pl.pallas_call(
        _attn_out_kernel,
        grid=(t // tb,),
        in_specs=[pl.BlockSpec((tb, 512), row), pl.BlockSpec((tb, 512), row), pl.BlockSpec((tb, D_MODEL), row),
                  pl.BlockSpec((D_MODEL, D_MODEL), const),
                  pl.BlockSpec((1, D_MODEL), const), pl.BlockSpec((D_MODEL, 2 * LANES), const),
                  pl.BlockSpec((D_MODEL, LANES), const)],
        out_specs=[pl.BlockSpec((tb, D_MODEL), row), pl.BlockSpec((tb, ROW_TILES, LANES), lambda i: (i, 0, 0)),
                   pl.BlockSpec((N_EXPERTS, tb), lambda i: (0, i))],
        out_shape=[jax.ShapeDtypeStruct((t, D_MODEL), f32), jax.ShapeDtypeStruct((t, ROW_TILES, LANES), f32),
                   jax.ShapeDtypeStruct((N_EXPERTS, t), f32)],
        compiler_params=_cparams(("arbitrary",)),
        name="attn_out",
    )(do, so, x2, w, g, rw, rhi)


def _select_kernel(a_ref, o_ref, *, capacity):
    bits = pltpu.bitcast(a_ref[...], jnp.int32)

    def count_ge(cand):
        return jnp.sum((bits >= cand).astype(f32), axis=1, keepdims=True)

    def body(i, thr):
        cand = thr | jnp.left_shift(jnp.int32(1), 30 - i)
        return jnp.where(count_ge(cand) >= capacity, cand, thr)

    thr = lax.fori_loop(0, 31, body, jnp.zeros((N_EXPERTS, 1), jnp.int32))
    need = capacity - count_ge(thr + 1)
    lane = _lane_iota((N_EXPERTS, LANES))
    o_ref[...] = jnp.where(lane == 0, pltpu.bitcast(thr, f32), need)


def _select(afft, capacity):
    return pl.pallas_call(
        functools.partial(_select_kernel, capacity=float(capacity)),
        out_shape=jax.ShapeDtypeStruct((N_EXPERTS, LANES), f32),
        compiler_params=pltpu.CompilerParams(vmem_limit_bytes=VMEM_LIMIT),
        name="select",
    )(afft)


def _rank_kernel(afft_ref, thr_ref, need_ref, tok_ref, rank_ref, offs_out_ref, ids_ref, carry_ref, offs_ref):
    c = pl.program_id(0)

    @pl.when(c == 0)
    def _():
        tok_ref[...] = jnp.zeros_like(tok_ref)
        carry_ref[...] = jnp.zeros_like(carry_ref)
        for e in range(N_EXPERTS):
            offs_ref[e] = 0

    a = afft_ref[...]
    thr = thr_ref[...]
    gt = a > thr
    eq = a == thr
    ri = lax.broadcasted_iota(jnp.int32, (RANK_CHUNK, RANK_CHUNK), 0)
    ci = lax.broadcasted_iota(jnp.int32, (RANK_CHUNK, RANK_CHUNK), 1)
    before = (ri < ci).astype(bf16)
    eq_rank = jnp.dot(eq.astype(bf16), before, preferred_element_type=f32) + carry_ref[:, 0:1]
    sel = gt | (eq & (eq_rank < need_ref[...]))
    local = jnp.dot(sel.astype(bf16), before, preferred_element_type=f32)
    local = jnp.where(sel, local, NOT_MEMBER)
    rank_ref[...] = local.T
    n_sel = jnp.sum(sel.astype(f32), axis=1, keepdims=True)
    carry_ref[:, 0:1] = carry_ref[:, 0:1] + jnp.sum(eq.astype(f32), axis=1, keepdims=True)

    sub = lax.broadcasted_iota(jnp.int32, (8, RANK_CHUNK), 0)
    loc = lax.broadcasted_iota(jnp.int32, (8, RANK_CHUNK), 1)
    ids = jnp.where(sub == 0, loc, jnp.where(sub == 1, 1, 0)).astype(f32)
    base = (c * RANK_CHUNK).astype(f32)
    offs = [offs_ref[e] for e in range(N_EXPERTS)]
    counts = [jnp.sum(n_sel[e:e + 1, :]).astype(jnp.int32) for e in range(N_EXPERTS)]
    reach = jnp.int32(0)
    for e in range(N_EXPERTS):
        reach = jnp.maximum(reach, offs[e] % LANES + counts[e])

    def place(n_tiles):
        slot = lax.broadcasted_iota(jnp.int32, (n_tiles * LANES, RANK_CHUNK), 0).astype(f32)
        for e in range(N_EXPERTS):
            j0 = offs[e] // LANES
            shift = (offs[e] - j0 * LANES).astype(f32)
            row_rank = local[e:e + 1, :]
            rel = jnp.where(row_rank >= 0.0, row_rank + shift, -1.0)
            g = a[e:e + 1, :]
            g_hi = g.astype(bf16).astype(f32)
            g_mid = (g - g_hi).astype(bf16).astype(f32)
            g_lo = ((g - g_hi) - g_mid).astype(bf16).astype(f32)
            lhs = jnp.where(sub == 2, g_hi, jnp.where(sub == 3, g_mid, jnp.where(sub == 4, g_lo, ids))).astype(bf16)
            onehot = jnp.where(slot == rel, 1.0, 0.0).astype(bf16)
            placed = lax.dot_general(lhs, onehot, (((1,), (1,)), ((), ())), preferred_element_type=f32)
            row = lax.broadcasted_iota(jnp.int32, placed.shape, 0)
            filled = jnp.broadcast_to(placed[1:2, :], placed.shape)
            placed = jnp.where(row == 0, placed + base * filled, placed)
            for k in range(n_tiles):
                tok_ref[e, j0 + k] = tok_ref[e, j0 + k] + placed[:, k * LANES:(k + 1) * LANES]

    @pl.when(reach <= 2 * LANES)
    def _():
        place(2)

    @pl.when(reach > 2 * LANES)
    def _():
        place(3)

    for e in range(N_EXPERTS):
        offs_ref[e] = offs[e] + counts[e]
        offs_out_ref[c * N_EXPERTS + e] = offs[e]
        offs_out_ref[(c + 1) * N_EXPERTS + e] = offs[e] + counts[e]

    @pl.when(c == pl.num_programs(0) - 1)
    def _():
        ids_ref[...] = tok_ref[:, :, 0, :].astype(jnp.int32)


def _rank(afft, thr, need, capacity):
    t = afft.shape[1]
    n_chunks = t // RANK_CHUNK
    n_tiles = capacity // LANES + 3
    const2 = lambda c: (0, 0)
    return pl.pallas_call(
        _rank_kernel,
        grid=(n_chunks,),
        in_specs=[pl.BlockSpec((N_EXPERTS, RANK_CHUNK), lambda c: (0, c)),
                  pl.BlockSpec((N_EXPERTS, 1), const2), pl.BlockSpec((N_EXPERTS, 1), const2)],
        out_specs=[pl.BlockSpec((N_EXPERTS, n_tiles, 8, LANES), lambda c: (0, 0, 0, 0)),
                   pl.BlockSpec((RANK_CHUNK, N_EXPERTS), lambda c: (c, 0)),
                   pl.BlockSpec(memory_space=pltpu.SMEM),
                   pl.BlockSpec((N_EXPERTS, n_tiles, LANES), lambda c: (0, 0, 0))],
        out_shape=[jax.ShapeDtypeStruct((N_EXPERTS, n_tiles, 8, LANES), f32),
                   jax.ShapeDtypeStruct((t, N_EXPERTS), f32),
                   jax.ShapeDtypeStruct(((n_chunks + 1) * N_EXPERTS,), jnp.int32),
                   jax.ShapeDtypeStruct((N_EXPERTS, n_tiles, LANES), jnp.int32)],
        scratch_shapes=[pltpu.VMEM((N_EXPERTS, LANES), f32), pltpu.SMEM((N_EXPERTS,), jnp.int32)],
        compiler_params=_cparams(("arbitrary",)),
        name="rank",
    )(afft, thr, need)


DMA_UNROLL = 8
FFN_COL_BLOCKS = 4
FFN_SLOTS = 3


def _ffn_kernel(tok_ref, nxt_ref, far_ref, lst_ref, rows_ref, wg_ref, wu_ref, wd_ref, o_ref, xbuf, sem):
    s = pl.program_id(0)
    n = pl.num_programs(0) - 1
    slot = s % FFN_SLOTS
    ahead = (s + 2) % FFN_SLOTS
    tc = xbuf.shape[1]

    def row_copy(idx_ref, k, r):
        return pltpu.make_async_copy(rows_ref.at[pl.ds(idx_ref[0, 0, r], 1)], xbuf.at[k, pl.ds(r, 1)], sem.at[k])

    def wait_rows(k):
        pltpu.make_async_copy(rows_ref.at[pl.ds(0, tc)], xbuf.at[k], sem.at[k]).wait()

    @pl.when(s == 0)
    def _():
        def body(r, carry):
            row_copy(tok_ref, 0, r).start()
            row_copy(nxt_ref, 1, r).start()
            return carry
        lax.fori_loop(0, tc, body, 0, unroll=DMA_UNROLL)

    @pl.when(s < n)
    def _():
        wait_rows(slot)
        per = tc // FFN_COL_BLOCKS

        def issue(group):
            for r in range(group * per, (group + 1) * per):
                row_copy(far_ref, ahead, r).start()

        cw = EXPERT_FF // FFN_COL_BLOCKS
        mids = []
        for b in range(FFN_COL_BLOCKS):
            issue(b)
            x = xbuf[slot].reshape(tc, D_MODEL).astype(bf16)
            g = jnp.dot(x, wg_ref[0, 0, :, b * cw:(b + 1) * cw], preferred_element_type=f32)
            u = jnp.dot(x, wu_ref[0, 0, :, b * cw:(b + 1) * cw], preferred_element_type=f32)
            mids.append((g * jax.nn.sigmoid(g) * u).astype(bf16))
        hmid = jnp.concatenate(mids, axis=1)
        lst = lst_ref[0]
        gates = lst[:, 2, :] + lst[:, 3, :] + lst[:, 4, :]
        gates_t = jnp.concatenate([gates, jnp.zeros((8 - gates.shape[0], LANES), f32)], axis=0).T
        gate = jnp.concatenate([gates_t[:, k:k + 1] for k in range(gates.shape[0])], axis=0)
        ow = D_MODEL // FFN_COL_BLOCKS
        for b in range(FFN_COL_BLOCKS):
            o_ref[:, b * ow:(b + 1) * ow] = (jnp.dot(hmid, wd_ref[0, 0, :, b * ow:(b + 1) * ow],
                                                     preferred_element_type=f32) * gate).astype(bf16)

    @pl.when(s == n)
    def _():
        wait_rows(slot)
        wait_rows((s + 1) % FFN_SLOTS)
        o_ref[...] = jnp.zeros_like(o_ref)


def _ffn(tok, lists, rows, wg, wu, wd, layer, capacity, tc):
    nt = capacity // tc
    lt = tc // LANES
    n = N_EXPERTS * nt
    expert = lambda s: (layer, jnp.minimum(s // nt, N_EXPERTS - 1), 0, 0)
    return pl.pallas_call(
        _ffn_kernel,
        grid=(n + 1,),
        in_specs=[pl.BlockSpec((1, 1, tc), lambda s: (jnp.minimum(s, n - 1), 0, 0), memory_space=pltpu.SMEM),
                  pl.BlockSpec((1, 1, tc), lambda s: (jnp.minimum(s + 1, n - 1), 0, 0), memory_space=pltpu.SMEM),
                  pl.BlockSpec((1, 1, tc), lambda s: (jnp.minimum(s + 2, n - 1), 0, 0), memory_space=pltpu.SMEM),
                  pl.BlockSpec((1, lt, 8, LANES), lambda s: (jnp.minimum(s // nt, N_EXPERTS - 1), s % nt, 0, 0)),
                  pl.BlockSpec(memory_space=pl.ANY),
                  pl.BlockSpec((1, 1, D_MODEL, EXPERT_FF), expert), pl.BlockSpec((1, 1, D_MODEL, EXPERT_FF), expert),
                  pl.BlockSpec((1, 1, EXPERT_FF, D_MODEL), expert)],
        out_specs=pl.BlockSpec((tc, D_MODEL), lambda s: (s, 0)),
        out_shape=jax.ShapeDtypeStruct(((n + 1) * tc, D_MODEL), bf16),
        scratch_shapes=[pltpu.VMEM((FFN_SLOTS, tc, ROW_TILES, LANES), f32), pltpu.SemaphoreType.DMA((FFN_SLOTS,))],
        compiler_params=_cparams(("arbitrary",)),
        name="ffn",
    )(tok, tok, tok, lists, rows, wg, wu, wd)


WIN = 128
WIN_SMALL = 64
ROW_ALIGN = 16
SPARE = 2


def _combine_kernel(offs_ref, h_ref, rank_ref, ye_ref, y_ref, win, sem, *, capacity):
    c = pl.program_id(0)
    n_chunks = pl.num_programs(0)
    slot = c % 2

    def window_starts(chunk):
        return [(offs_ref[chunk * N_EXPERTS + e] // ROW_ALIGN) * ROW_ALIGN for e in range(N_EXPERTS)]

    def reach(chunk, starts):
        far = jnp.int32(0)
        for e in range(N_EXPERTS):
            st = offs_ref[chunk * N_EXPERTS + e]
            cnt = offs_ref[(chunk + 1) * N_EXPERTS + e] - st
            far = jnp.maximum(far, jnp.where(cnt > 0, st - starts[e] + cnt, 0))
        return far

    def copy(e, start, p, k, w):
        src = jnp.minimum(e * capacity + start + p * w, ye_ref.shape[0] - w)
        return pltpu.make_async_copy(ye_ref.at[pl.ds(pl.multiple_of(src, ROW_ALIGN), w)],
                                     win.at[k, pl.ds(e * w, w)], sem.at[k])

    def fetch_first(chunk, k):
        st = window_starts(chunk)
        far = reach(chunk, st)
        for w, cond in ((WIN_SMALL, far <= WIN_SMALL), (WIN, far > WIN_SMALL)):
            @pl.when(cond)
            def _(w=w):
                for e in range(N_EXPERTS):
                    copy(e, st[e], 0, k, w).start()

    @pl.when(c == 0)
    def _():
        fetch_first(c, slot)

    @pl.when(c + 1 < n_chunks)
    def _():
        fetch_first(c + 1, 1 - slot)

    starts = window_starts(c)
    far = reach(c, starts)

    def scatter(w):
        width = N_EXPERTS * w
        col = lax.broadcasted_iota(jnp.int32, (N_EXPERTS, width), 1)
        row = lax.broadcasted_iota(jnp.int32, (N_EXPERTS, width), 0)
        spread = jnp.where((col // w) == row, 1.0, 0.0).astype(bf16)
        rank_rep = jnp.dot(rank_ref[...].astype(bf16), spread, preferred_element_type=f32)
        col1 = lax.broadcasted_iota(jnp.int32, (1, width), 1)
        base = jnp.zeros((1, width), jnp.int32)
        for e in range(N_EXPERTS):
            base = jnp.where((col1 // w) == e, starts[e] - offs_ref[c * N_EXPERTS + e], base)
        base = (base + col1 % w).astype(f32)

        def scatter_matmul(p, k):
            hit = rank_rep == base + (p * w).astype(f32)
            sel = jnp.where(hit, 1.0, 0.0).astype(bf16)
            return jnp.dot(sel, win[k, 0:width, :], preferred_element_type=f32)

        def fetch_pass(p):
            for e in range(N_EXPERTS):
                copy(e, starts[e], p, SPARE, w).start()

        def wait_windows(k):
            for e in range(N_EXPERTS):
                copy(e, starts[e], 0, k, w).wait()

        if w == WIN_SMALL:
            wait_windows(slot)
            y_ref[...] = h_ref[...] + scatter_matmul(jnp.int32(0), slot)
            return
        n_pass = (far + w - 1) // w

        @pl.when(n_pass > 1)
        def _():
            fetch_pass(jnp.int32(1))

        wait_windows(slot)
        acc = scatter_matmul(jnp.int32(0), slot)

        def later_pass(p, acc):
            wait_windows(SPARE)
            res = scatter_matmul(p, SPARE)

            @pl.when(p + 1 < n_pass)
            def _():
                fetch_pass(p + 1)
            return acc + res

        y_ref[...] = h_ref[...] + lax.fori_loop(1, n_pass, later_pass, acc)

    @pl.when(far <= WIN_SMALL)
    def _():
        scatter(WIN_SMALL)

    @pl.when(far > WIN_SMALL)
    def _():
        scatter(WIN)


def _combine(offs, h, rank16, ye, capacity):
    t = h.shape[0]
    row = lambda c, offs: (c, 0)
    return pl.pallas_call(
        functools.partial(_combine_kernel, capacity=capacity),
        grid_spec=pltpu.PrefetchScalarGridSpec(
            num_scalar_prefetch=1,
            grid=(t // RANK_CHUNK,),
            in_specs=[pl.BlockSpec((RANK_CHUNK, D_MODEL), row), pl.BlockSpec((RANK_CHUNK, N_EXPERTS), row),
                      pl.BlockSpec(memory_space=pl.ANY)],
            out_specs=pl.BlockSpec((RANK_CHUNK, D_MODEL), row),
            scratch_shapes=[pltpu.VMEM((SPARE + 1, N_EXPERTS * WIN, D_MODEL), bf16),
                            pltpu.SemaphoreType.DMA((SPARE + 1,))]),
        out_shape=jax.ShapeDtypeStruct((t, D_MODEL), f32),
        compiler_params=_cparams(("arbitrary",)),
        name="combine",
    )(offs, h, rank16, ye)


def _rope_tables(seq):
    half = HEAD_DIM // 2
    inv = 1.0 / (ROPE_THETA ** (jnp.arange(0, HEAD_DIM, 2, dtype=f32) / HEAD_DIM))
    ang = jnp.arange(seq, dtype=f32)[:, None] * inv[None, :]
    cos = jnp.tile(jnp.cos(ang), (1, LANES // half))
    sign = jnp.where((jnp.arange(LANES) % HEAD_DIM) < half, -1.0, 1.0).astype(f32)
    sin = jnp.tile(jnp.sin(ang), (1, LANES // half)) * sign[None, :]
    return cos, sin


def _block_diag_mean(width):
    i = jnp.arange(width)
    return jnp.where((i[:, None] // HEAD_DIM) == (i[None, :] // HEAD_DIM), 1.0 / HEAD_DIM, 0.0).astype(bf16)


def _pick_tile(n, pref):
    while n % pref:
        pref //= 2
    return pref


def _trunk(x, p):
    batch, seq, _ = x.shape
    t = batch * seq
    capacity = CAPACITY_FACTOR * t // N_EXPERTS
    assert seq % (2 * BLOCK) == 0 and t % RANK_CHUNK == 0 and capacity % LANES == 0
    tb = _pick_tile(seq, 512)
    tq = _pick_tile(seq, 256)
    kc = _pick_tile(seq, DIFF_KEY_CHUNK)
    heads = DIFF_HEADS
    tc = _pick_tile(capacity, FFN_TILE)
    cos, sin = _rope_tables(seq)
    gm256, gm128 = _block_diag_mean(256), _block_diag_mean(128)
    x2 = x.reshape(t, D_MODEL)
    depth = p["w_in"].shape[0]
    for l in range(depth):
        lam_init = 0.8 - 0.6 * math.exp(-0.3 * l)
        lam = (jnp.exp(jnp.sum(p["lambda_q1"][l] * p["lambda_k1"][l]))
               - jnp.exp(jnp.sum(p["lambda_q2"][l] * p["lambda_k2"][l])) + lam_init)
        bound = (8.0 * LOG2E * (1.0 + 2.0 ** -6)) * jnp.max(jnp.abs(p["diff_q_norm"][l])) \
            * jnp.max(jnp.abs(p["diff_k_norm"][l]))
        scal = jnp.stack([lam, jnp.asarray(1.0 - lam_init, f32), bound]).astype(f32)
        tile8 = lambda v: jnp.tile(v, 8)[None, :]
        dq, dk, dv, sqp, sk, sv = _attn_in(
            x2, p["attn_norm"][l][None, :], p["w_in_bf"][l], cos, sin,
            tile8(p["diff_q_norm"][l]), tile8(p["diff_k_norm"][l]), tile8(p["swa_q_norm"][l]),
            jnp.tile(p["swa_k_norm"][l], 2)[None, :], gm256, gm128, seq, tb)
        do = _diff_attn(scal, dq, dk, dv, p["diff_subln"][l][None, :], batch, seq, tq, heads, kc)
        swa_bound = (8.0 * LOG2E * (1.0 + 2.0 ** -6)) * jnp.max(jnp.abs(p["swa_q_norm"][l])) \
            * jnp.max(jnp.abs(p["swa_k_norm"][l]))
        so = _band_attn(jnp.concatenate([p["swa_sink"][l] * LOG2E, swa_bound[None]]).astype(f32), sqp, sk, sv, batch, seq)
        h, rows, afft = _attn_out(do, so, x2, p["w_out_bf"][l], p["ffn_norm"][l][None, :],
                                   jnp.concatenate([p["r_hi"][l], p["r_lo"][l]], axis=1), p["r_hi"][l], tb)
        sel = _select(afft, capacity)
        tokbuf, rank16, offs, ids = _rank(afft, sel[:, 0:1], sel[:, 1:2], capacity)
        tok = ids[:, :capacity // LANES].reshape(N_EXPERTS * capacity // tc, 1, tc)
        ye = _ffn(tok, tokbuf, rows, p["w_gate_bf"], p["w_up_bf"], p["w_down_bf"], l, capacity, tc)
        x2 = _combine(offs, h, rank16, ye, capacity)
    return x2.reshape(batch, seq, D_MODEL)


def kernel(x_prompt, x_sample, attn_norm, w_in, diff_q_norm, diff_k_norm, lambda_q1, lambda_k1, lambda_q2, lambda_k2,
           diff_subln, swa_q_norm, swa_k_norm, swa_sink, w_out, ffn_norm, w_router, w_gate, w_up, w_down):
    r_pad = jnp.pad(w_router, ((0, 0), (0, 0), (0, LANES - N_EXPERTS)))
    r_hi = r_pad.astype(bf16)
    p = dict(attn_norm=attn_norm, diff_q_norm=diff_q_norm, diff_k_norm=diff_k_norm, lambda_q1=lambda_q1,
             lambda_k1=lambda_k1, lambda_q2=lambda_q2, lambda_k2=lambda_k2, diff_subln=diff_subln,
             swa_q_norm=swa_q_norm, swa_k_norm=swa_k_norm, swa_sink=swa_sink, ffn_norm=ffn_norm,
             w_in=w_in, w_in_bf=w_in.astype(bf16), w_out_bf=w_out.astype(bf16),
             r_hi=r_hi, r_lo=(r_pad - r_hi.astype(f32)).astype(bf16),
             w_gate_bf=w_gate.astype(bf16), w_up_bf=w_up.astype(bf16), w_down_bf=w_down.astype(bf16))
    return (_trunk(x_prompt, p), _trunk(x_sample, p))
```

```python
import functools
import math

import jax
import jax.numpy as jnp
from jax import lax
from jax.experimental import pallas as pl
from jax.experimental.pallas import tpu as pltpu

D_MODEL = 1024
HEAD_DIM = 64
DIFF_HEADS = 4
DIFF_VDIM = 128
DIFF_WIDTH = 512
SWA_HEADS = 8
SWA_KV_HEADS = 2
SWA_GROUP = 4
SWA_WIDTH = 512
BLOCK = 128
N_EXPERTS = 16
EXPERT_FF = 1024
CAPACITY_FACTOR = 2
ROPE_THETA = 10000.0
EPS = 1e-6
LOG2E = 1.4426950408889634
EXP_SHIFT_LIMIT = 60.0
DIFF_KEY_CHUNK = 256
IN_WIDTH = 2304
LANES = 128
ROW_TILES = D_MODEL // LANES
RANK_CHUNK = 256
NOT_MEMBER = -256.0
FFN_TILE = 512
VMEM_LIMIT = 56 * 1024 * 1024

f32 = jnp.float32
bf16 = jnp.bfloat16


def _cparams(sem):
    return pltpu.CompilerParams(dimension_semantics=sem, vmem_limit_bytes=VMEM_LIMIT)


def _lane_iota(shape):
    return lax.broadcasted_iota(jnp.int32, shape, len(shape) - 1)


def _head_norm_rope(u, gain, gmat, cos, sin_signed):
    w = u.shape[1]
    gw = gmat.shape[0]
    u2 = (u * u).astype(bf16)
    ms = jnp.concatenate([jnp.dot(u2[:, i:i + gw], gmat, preferred_element_type=f32) for i in range(0, w, gw)], axis=1)
    un = u * lax.rsqrt(ms + EPS) * gain
    lane = _lane_iota(un.shape)
    first_half = (lane % HEAD_DIM) < (HEAD_DIM // 2)
    rot = jnp.where(first_half, pltpu.roll(un, w - HEAD_DIM // 2, 1), pltpu.roll(un, HEAD_DIM // 2, 1))
    return un * cos + rot * sin_signed


def _attn_in_kernel(x_ref, g_ref, w_ref, cos_ref, sin_ref, gdq_ref, gdk_ref, gsq_ref, gsk_ref, gm256_ref, gm128_ref,
                    dq_ref, dk_ref, dv_ref, sq_ref, sk_ref, sv_ref):
    x = x_ref[...]
    ms = jnp.mean(x * x, axis=-1, keepdims=True)
    xn = (x * lax.rsqrt(ms + EPS) * g_ref[...]).astype(bf16)
    proj = lambda lo, hi: jnp.dot(xn, w_ref[:, lo:hi], preferred_element_type=f32)
    cos1, sin1 = cos_ref[...], sin_ref[...]
    cos4 = jnp.concatenate([cos1] * 4, axis=1)
    sin4 = jnp.concatenate([sin1] * 4, axis=1)
    gm256 = gm256_ref[...]
    scale = (1.0 / math.sqrt(HEAD_DIM)) * LOG2E
    dq = _head_norm_rope(proj(0, 512), gdq_ref[...], gm256, cos4, sin4)
    dk = _head_norm_rope(proj(512, 1024), gdk_ref[...], gm256, cos4, sin4)
    sq = _head_norm_rope(proj(1536, 2048), gsq_ref[...], gm256, cos4, sin4)
    skv = proj(2048, 2304)
    sk = _head_norm_rope(skv[:, 0:128], gsk_ref[...], gm128_ref[...], cos1, sin1)
    dq_ref[...] = (dq * scale).astype(bf16)
    dk_ref[...] = dk.astype(bf16)
    dv_ref[...] = proj(1024, 1536).astype(bf16)
    sk_ref[...] = sk.astype(bf16)
    sv_ref[...] = skv[:, 128:256].astype(bf16)
    sq = sq * scale
    lane = _lane_iota((sq.shape[0], LANES))
    for j in range(SWA_HEADS):
        blk = sq[:, (j // 2) * LANES:(j // 2 + 1) * LANES]
        hk = j // SWA_GROUP
        if (j % 2) != hk:
            blk = pltpu.roll(blk, HEAD_DIM, 1)
        blk = jnp.where((lane // HEAD_DIM) == hk, blk, 0.0)
        sq_ref[:, j * LANES:(j + 1) * LANES] = blk.astype(bf16)


def _attn_in(x2, g, w_in, cos, sin, gdq, gdk, gsq, gsk, gm256, gm128, seq, tb):
    t = x2.shape[0]
    nb_seq = seq // tb
    row = lambda i: (i, 0)
    const = lambda i: (0, 0)
    outs = [jax.ShapeDtypeStruct((t, 512), bf16)] * 3 + [jax.ShapeDtypeStruct((t, 1024), bf16)] + \
           [jax.ShapeDtypeStruct((t, 128), bf16)] * 2
    return pl.pallas_call(
        _attn_in_kernel,
        grid=(t // tb,),
        in_specs=[
            pl.BlockSpec((tb, D_MODEL), row),
            pl.BlockSpec((1, D_MODEL), const),
            pl.BlockSpec((D_MODEL, IN_WIDTH), const),
            pl.BlockSpec((tb, LANES), lambda i: (i % nb_seq, 0)),
            pl.BlockSpec((tb, LANES), lambda i: (i % nb_seq, 0)),
            pl.BlockSpec((1, 512), const), pl.BlockSpec((1, 512), const), pl.BlockSpec((1, 512), const),
            pl.BlockSpec((1, 128), const),
            pl.BlockSpec((256, 256), const), pl.BlockSpec((128, 128), const),
        ],
        out_specs=[pl.BlockSpec((tb, 512), row)] * 3 + [pl.BlockSpec((tb, 1024), row)] + [pl.BlockSpec((tb, 128), row)] * 2,
        out_shape=outs,
        compiler_params=_cparams(("arbitrary",)),
        name="attn_in",
    )(x2, g, w_in, cos, sin, gdq, gdk, gsq, gsk, gm256, gm128)


def _diff_attn_kernel(scal_ref, q_ref, k_ref, v_ref, g_ref, o_ref, *, kc):
    lam = scal_ref[0]
    out_scale = scal_ref[1]
    bound = scal_ref[2]
    tq = q_ref.shape[0]
    n_heads = q_ref.shape[1] // LANES
    n_chunks = k_ref.shape[0] // kc
    lane = _lane_iota((tq, LANES))

    def stacked_q(h):
        q = q_ref[:, h * LANES:(h + 1) * LANES]
        zero = jnp.zeros_like(q)
        return jnp.concatenate([jnp.where(lane < HEAD_DIM, q, zero), jnp.where(lane >= HEAD_DIM, q, zero)], axis=0)

    def value_block(rows, h):
        vblk = v_ref[rows, h * LANES:(h + 1) * LANES]
        return jnp.concatenate([vblk, jnp.ones_like(vblk)], axis=1)

    def finish(ev, h):
        on = ev[:, 0:LANES] * (1.0 / ev[:, LANES:LANES + 1])
        o = on[:tq] - lam * on[tq:]
        ms = jnp.mean(o * o, axis=-1, keepdims=True)
        o_ref[:, h * LANES:(h + 1) * LANES] = ((o * lax.rsqrt(ms + EPS) * g_ref[...]) * out_scale).astype(bf16)

    @pl.when(bound <= EXP_SHIFT_LIMIT)
    def _():
        for h in range(n_heads):
            qs = stacked_q(h)
            ev = jnp.zeros((2 * tq, 2 * LANES), f32)
            for c in range(n_chunks):
                rows = slice(c * kc, (c + 1) * kc)
                s = lax.dot_general(qs, k_ref[rows, h * LANES:(h + 1) * LANES], (((1,), (1,)), ((), ())),
                                    preferred_element_type=f32)
                ev = ev + jnp.dot(jnp.exp2(s - bound).astype(bf16), value_block(rows, h), preferred_element_type=f32)
            finish(ev, h)

    @pl.when(bound > EXP_SHIFT_LIMIT)
    def _():
        for h in range(n_heads):
            qs = stacked_q(h)

            def chunk(c, carry, qs=qs, h=h):
                m, ev = carry
                rows = pl.ds(pl.multiple_of(c * kc, kc), kc)
                s = lax.dot_general(qs, k_ref[rows, h * LANES:(h + 1) * LANES], (((1,), (1,)), ((), ())),
                                    preferred_element_type=f32)
                m_new = jnp.maximum(m, jnp.max(s, axis=-1, keepdims=True))
                ev = jnp.exp2(m - m_new) * ev + jnp.dot(jnp.exp2(s - m_new).astype(bf16), value_block(rows, h),
                                                        preferred_element_type=f32)
                return m_new, ev

            init = (jnp.full((2 * tq, 1), -jnp.inf, f32), jnp.zeros((2 * tq, 2 * LANES), f32))
            finish(lax.fori_loop(0, n_chunks, chunk, init)[1], h)


def _diff_attn(scal, dq, dk, dv, g, batch, seq, tq, heads, kc):
    t = dq.shape[0]
    nq = seq // tq
    w = heads * LANES
    return pl.pallas_call(
        functools.partial(_diff_attn_kernel, kc=kc),
        grid=(batch, DIFF_HEADS // heads, nq),
        in_specs=[
            pl.BlockSpec(memory_space=pltpu.SMEM),
            pl.BlockSpec((tq, w), lambda b, h, i: (b * nq + i, h)),
            pl.BlockSpec((seq, w), lambda b, h, i: (b, h)),
            pl.BlockSpec((seq, w), lambda b, h, i: (b, h)),
            pl.BlockSpec((1, LANES), lambda b, h, i: (0, 0)),
        ],
        out_specs=pl.BlockSpec((tq, w), lambda b, h, i: (b * nq + i, h)),
        out_shape=jax.ShapeDtypeStruct((t, DIFF_WIDTH), bf16),
        compiler_params=_cparams(("arbitrary", "arbitrary", "arbitrary")),
        name="diff_attn",
    )(scal, dq, dk, dv, g)


def _band_attn_kernel(scal_ref, q_ref, kp_ref, kc_ref, kn_ref, vp_ref, vc_ref, vn_ref, o_ref):
    m_blk = pl.program_id(1)
    last = pl.num_programs(1) - 1
    bound = scal_ref[SWA_HEADS]
    kc, vc = kc_ref[...], vc_ref[...]
    r = lax.broadcasted_iota(jnp.int32, (BLOCK, 3 * BLOCK), 0)
    c = lax.broadcasted_iota(jnp.int32, (BLOCK, 3 * BLOCK), 1)
    band = (c >= r) & (c <= r + 2 * BLOCK)
    lane = _lane_iota((BLOCK, LANES))
    grp = lax.broadcasted_iota(jnp.int32, (SWA_GROUP * BLOCK, 1), 0) // BLOCK

    def run(streaming):
        halves = (
            (0, kp_ref[...], kc, vp_ref[...], vc, band & ((c >= BLOCK) | (m_blk > 0))),
            (1, kc, kn_ref[...], vc, vn_ref[...], band & ((c < 2 * BLOCK) | (m_blk < last))),
        )
        for half, ka, kb, va, vb, valid in halves:
            k3 = jnp.concatenate([ka, kb], axis=0)
            v3 = jnp.concatenate([va, vb], axis=0)
            v31 = jnp.concatenate([v3, jnp.ones_like(v3)], axis=1)
            valid4 = jnp.concatenate([valid] * SWA_GROUP, axis=0)
            rows = slice(half * BLOCK, (half + 1) * BLOCK)
            outs = []
            for hk in range(SWA_KV_HEADS):
                qs = jnp.concatenate([q_ref[rows, (hk * SWA_GROUP + g) * LANES:(hk * SWA_GROUP + g + 1) * LANES]
                                      for g in range(SWA_GROUP)], axis=0)
                s = lax.dot_general(qs, k3, (((1,), (1,)), ((), ())), preferred_element_type=f32)
                sink = jnp.zeros((SWA_GROUP * BLOCK, 1), f32)
                for g in range(SWA_GROUP):
                    sink = jnp.where(grp == g, scal_ref[hk * SWA_GROUP + g], sink)
                if streaming:
                    m = jnp.maximum(sink, bound)
                    e = jnp.where(valid4, jnp.exp2(s - m), 0.0)
                else:
                    s = jnp.where(valid4, s, -1e30)
                    m = jnp.maximum(jnp.max(s, axis=-1, keepdims=True), sink)
                    e = jnp.exp2(s - m)
                ev = jnp.dot(e.astype(bf16), v31, preferred_element_type=f32)
                o = ev[:, 0:LANES] * (1.0 / (ev[:, LANES:LANES + 1] + jnp.exp2(sink - m)))
                outs.extend(o[g * BLOCK:(g + 1) * BLOCK] for g in range(SWA_GROUP))
            for pc in range(SWA_HEADS // 2):
                hk = (2 * pc) // SWA_GROUP
                left, right = outs[2 * pc], outs[2 * pc + 1]
                if hk == 1:
                    left = pltpu.roll(left, HEAD_DIM, 1)
                else:
                    right = pltpu.roll(right, HEAD_DIM, 1)
                o_ref[rows, pc * LANES:(pc + 1) * LANES] = jnp.where(lane < HEAD_DIM, left, right).astype(bf16)

    @pl.when(bound <= EXP_SHIFT_LIMIT)
    def _():
        run(True)

    @pl.when(bound > EXP_SHIFT_LIMIT)
    def _():
        run(False)


def _band_attn(scal, sqp, sk, sv, batch, seq):
    t = sqp.shape[0]
    nb = seq // BLOCK
    nb2 = nb // 2
    prev = lambda b, m: (b * nb + jnp.maximum(2 * m - 1, 0), 0)
    pair = lambda b, m: (b * nb2 + m, 0)
    nxt = lambda b, m: (b * nb + jnp.minimum(2 * m + 2, nb - 1), 0)
    one = lambda f: pl.BlockSpec((BLOCK, LANES), f)
    two = pl.BlockSpec((2 * BLOCK, LANES), pair)
    return pl.pallas_call(
        _band_attn_kernel,
        grid=(batch, nb2),
        in_specs=[pl.BlockSpec(memory_space=pltpu.SMEM), pl.BlockSpec((2 * BLOCK, SWA_HEADS * LANES), pair),
                  one(prev), two, one(nxt), one(prev), two, one(nxt)],
        out_specs=pl.BlockSpec((2 * BLOCK, SWA_WIDTH), pair),
        out_shape=jax.ShapeDtypeStruct((t, SWA_WIDTH), bf16),
        compiler_params=_cparams(("arbitrary", "arbitrary")),
        name="band_attn",
    )(scal, sqp, sk, sk, sk, sv, sv, sv)


def _attn_out_kernel(do_ref, so_ref, x_ref, w_ref, g_ref, rw_ref, rhi_ref, h_ref, hn_ref, afft_ref):
    mix = jnp.concatenate([do_ref[...], so_ref[...]], axis=1)
    h = x_ref[...] + jnp.dot(mix, w_ref[...], preferred_element_type=f32)
    h_ref[...] = h
    ms = jnp.mean(h * h, axis=-1, keepdims=True)
    hn = h * lax.rsqrt(ms + EPS) * g_ref[...]
    hn_ref[...] = hn.reshape(hn.shape[0], ROW_TILES, LANES)
    xh = hn.astype(bf16)
    xl = (hn - xh.astype(f32)).astype(bf16)
    both = jnp.dot(xh, rw_ref[...], preferred_element_type=f32)
    logits = both[:, 0:LANES] + (both[:, LANES:2 * LANES] + jnp.dot(xl, rhi_ref[...], preferred_element_type=f32))
    lane = _lane_iota(logits.shape)
    live = lane < N_EXPERTS
    lm = jnp.where(live, logits, -1e30)
    m = jnp.max(lm, axis=-1, keepdims=True)
    e = jnp.where(live, jnp.exp(lm - m), 0.0)
    aff = e / jnp.sum(e, axis=-1, keepdims=True)
    afft_ref[...] = aff.T[0:N_EXPERTS, :]


def _attn_out(do, so, x2, w, g, rw, rhi, tb):
    t = x2.shape[0]
    row = lambda i: (i, 0)
    const = lambda i: (0, 0)
    return pl.pallas_call(
        _attn_out_kernel,
        grid=(t // tb,),
        in_specs=[pl.BlockSpec((tb, 512), row), pl.BlockSpec((tb, 512), row), pl.BlockSpec((tb, D_MODEL), row),
                  pl.BlockSpec((D_MODEL, D_MODEL), const),
                  pl.BlockSpec((1, D_MODEL), const), pl.BlockSpec((D_MODEL, 2 * LANES), const),
                  pl.BlockSpec((D_MODEL, LANES), const)],
        out_specs=[pl.BlockSpec((tb, D_MODEL), row), pl.BlockSpec((tb, ROW_TILES, LANES), lambda i: (i, 0, 0)),
                   pl.BlockSpec((N_EXPERTS, tb), lambda i: (0, i))],
        out_shape=[jax.ShapeDtypeStruct((t, D_MODEL), f32), jax.ShapeDtypeStruct((t, ROW_TILES, LANES), f32),
                   jax.ShapeDtypeStruct((N_EXPERTS, t), f32)],
        compiler_params=_cparams(("arbitrary",)),
        name="attn_out",
    )(do, so, x2, w, g, rw, rhi)


def _select_kernel(a_ref, o_ref, *, capacity):
    bits = pltpu.bitcast(a_ref[...], jnp.int32)

    def count_ge(cand):
        return jnp.sum((bits >= cand).astype(f32), axis=1, keepdims=True)

    def body(i, thr):
        cand = thr | jnp.left_shift(jnp.int32(1), 30 - i)
        return jnp.where(count_ge(cand) >= capacity, cand, thr)

    thr = lax.fori_loop(0, 31, body, jnp.zeros((N_EXPERTS, 1), jnp.int32))
    need = capacity - count_ge(thr + 1)
    lane = _lane_iota((N_EXPERTS, LANES))
    o_ref[...] = jnp.where(lane == 0, pltpu.bitcast(thr, f32), need)


def _select(afft, capacity):
    return pl.pallas_call(
        functools.partial(_select_kernel, capacity=float(capacity)),
        out_shape=jax.ShapeDtypeStruct((N_EXPERTS, LANES), f32),
        compiler_params=pltpu.CompilerParams(vmem_limit_bytes=VMEM_LIMIT),
        name="select",
    )(afft)


def _rank_kernel(afft_ref, thr_ref, need_ref, tok_ref, rank_ref, offs_out_ref, ids_ref, carry_ref, offs_ref):
    c = pl.program_id(0)

    @pl.when(c == 0)
    def _():
        tok_ref[...] = jnp.zeros_like(tok_ref)
        carry_ref[...] = jnp.zeros_like(carry_ref)
        for e in range(N_EXPERTS):
            offs_ref[e] = 0

    a = afft_ref[...]
    thr = thr_ref[...]
    gt = a > thr
    eq = a == thr
    ri = lax.broadcasted_iota(jnp.int32, (RANK_CHUNK, RANK_CHUNK), 0)
    ci = lax.broadcasted_iota(jnp.int32, (RANK_CHUNK, RANK_CHUNK), 1)
    before = (ri < ci).astype(bf16)
    eq_rank = jnp.dot(eq.astype(bf16), before, preferred_element_type=f32) + carry_ref[:, 0:1]
    sel = gt | (eq & (eq_rank < need_ref[...]))
    local = jnp.dot(sel.astype(bf16), before, preferred_element_type=f32)
    local = jnp.where(sel, local, NOT_MEMBER)
    rank_ref[...] = local.T
    n_sel = jnp.sum(sel.astype(f32), axis=1, keepdims=True)
    carry_ref[:, 0:1] = carry_ref[:, 0:1] + jnp.sum(eq.astype(f32), axis=1, keepdims=True)

    sub = lax.broadcasted_iota(jnp.int32, (8, RANK_CHUNK), 0)
    loc = lax.broadcasted_iota(jnp.int32, (8, RANK_CHUNK), 1)
    ids = jnp.where(sub == 0, loc, jnp.where(sub == 1, 1, 0)).astype(f32)
    base = (c * RANK_CHUNK).astype(f32)
    offs = [offs_ref[e] for e in range(N_EXPERTS)]
    counts = [jnp.sum(n_sel[e:e + 1, :]).astype(jnp.int32) for e in range(N_EXPERTS)]
    reach = jnp.int32(0)
    for e in range(N_EXPERTS):
        reach = jnp.maximum(reach, (offs[e] & (LANES - 1)) + counts[e])

    def place(n_tiles):
        slot = lax.broadcasted_iota(jnp.int32, (n_tiles * LANES, RANK_CHUNK), 0).astype(f32)
        for e in range(N_EXPERTS):
            j0 = lax.shift_right_logical(offs[e], LANES.bit_length() - 1)
            shift = (offs[e] - j0 * LANES).astype(f32)
            row_rank = local[e:e + 1, :]
            rel = jnp.where(row_rank >= 0.0, row_rank + shift, -1.0)
            g = a[e:e + 1, :]
            g_hi = g.astype(bf16).astype(f32)
            g_mid = (g - g_hi).astype(bf16).astype(f32)
            g_lo = ((g - g_hi) - g_mid).astype(bf16).astype(f32)
            lhs = jnp.where(sub == 2, g_hi, jnp.where(sub == 3, g_mid, jnp.where(sub == 4, g_lo, ids))).astype(bf16)
            onehot = jnp.where(slot == rel, 1.0, 0.0).astype(bf16)
            placed = lax.dot_general(lhs, onehot, (((1,), (1,)), ((), ())), preferred_element_type=f32)
            row = lax.broadcasted_iota(jnp.int32, placed.shape, 0)
            filled = jnp.broadcast_to(placed[1:2, :], placed.shape)
            placed = jnp.where(row == 0, placed + base * filled, placed)
            for k in range(n_tiles):
                tok_ref[e, j0 + k] = tok_ref[e, j0 + k] + placed[:, k * LANES:(k + 1) * LANES]

    @pl.when(reach <= 2 * LANES)
    def _():
        place(2)

    @pl.when(reach > 2 * LANES)
    def _():
        place(3)

    for e in range(N_EXPERTS):
        offs_ref[e] = offs[e] + counts[e]
        offs_out_ref[c * N_EXPERTS + e] = offs[e]
        offs_out_ref[(c + 1) * N_EXPERTS + e] = offs[e] + counts[e]

    @pl.when(c == pl.num_programs(0) - 1)
    def _():
        ids_ref[...] = tok_ref[:, :, 0, :].astype(jnp.int32)


def _rank(afft, thr, need, capacity):
    t = afft.shape[1]
    n_chunks = t // RANK_CHUNK
    n_tiles = capacity // LANES + 3
    const2 = lambda c: (0, 0)
    return pl.pallas_call(
        _rank_kernel,
        grid=(n_chunks,),
        in_specs=[pl.BlockSpec((N_EXPERTS, RANK_CHUNK), lambda c: (0, c)),
                  pl.BlockSpec((N_EXPERTS, 1), const2), pl.BlockSpec((N_EXPERTS, 1), const2)],
        out_specs=[pl.BlockSpec((N_EXPERTS, n_tiles, 8, LANES), lambda c: (0, 0, 0, 0)),
                   pl.BlockSpec((RANK_CHUNK, N_EXPERTS), lambda c: (c, 0)),
                   pl.BlockSpec(memory_space=pltpu.SMEM),
                   pl.BlockSpec((N_EXPERTS, n_tiles, LANES), lambda c: (0, 0, 0))],
        out_shape=[jax.ShapeDtypeStruct((N_EXPERTS, n_tiles, 8, LANES), f32),
                   jax.ShapeDtypeStruct((t, N_EXPERTS), f32),
                   jax.ShapeDtypeStruct(((n_chunks + 1) * N_EXPERTS,), jnp.int32),
                   jax.ShapeDtypeStruct((N_EXPERTS, n_tiles, LANES), jnp.int32)],
        scratch_shapes=[pltpu.VMEM((N_EXPERTS, LANES), f32), pltpu.SMEM((N_EXPERTS,), jnp.int32)],
        compiler_params=_cparams(("arbitrary",)),
        name="rank",
    )(afft, thr, need)


DMA_UNROLL = 8
FFN_COL_BLOCKS = 4
FFN_SLOTS = 3


def _ffn_kernel(tok_ref, nxt_ref, far_ref, lst_ref, rows_ref, wg_ref, wu_ref, wd_ref, o_ref, xbuf, sem):
    s = pl.program_id(0)
    n = pl.num_programs(0) - 1
    slot = s % FFN_SLOTS
    ahead = (s + 2) % FFN_SLOTS
    tc = xbuf.shape[1]

    def row_copy(idx_ref, k, r):
        return pltpu.make_async_copy(rows_ref.at[pl.ds(idx_ref[0, 0, r], 1)], xbuf.at[k, pl.ds(r, 1)], sem.at[k])

    def wait_rows(k):
        pltpu.make_async_copy(rows_ref.at[pl.ds(0, tc)], xbuf.at[k], sem.at[k]).wait()

    @pl.when(s == 0)
    def _():
        def body(r, carry):
            row_copy(tok_ref, 0, r).start()
            row_copy(nxt_ref, 1, r).start()
            return carry
        lax.fori_loop(0, tc, body, 0, unroll=DMA_UNROLL)

    @pl.when(s < n)
    def _():
        wait_rows(slot)
        x = xbuf[slot].reshape(tc, D_MODEL).astype(bf16)
        per = tc // (2 * FFN_COL_BLOCKS)

        def issue(group):
            for r in range(group * per, (group + 1) * per):
                row_copy(far_ref, ahead, r).start()

        cw = EXPERT_FF // FFN_COL_BLOCKS
        mids = []
        for b in range(FFN_COL_BLOCKS):
            issue(b)
            g = jnp.dot(x, wg_ref[0, 0, :, b * cw:(b + 1) * cw], preferred_element_type=f32)
            u = jnp.dot(x, wu_ref[0, 0, :, b * cw:(b + 1) * cw], preferred_element_type=f32)
            mids.append((g * jax.nn.sigmoid(g) * u).astype(bf16))
        hmid = jnp.concatenate(mids, axis=1)
        lst = lst_ref[0]
        gates = lst[:, 2, :] + lst[:, 3, :] + lst[:, 4, :]
        gates_t = jnp.concatenate([gates, jnp.zeros((8 - gates.shape[0], LANES), f32)], axis=0).T
        gate = jnp.concatenate([gates_t[:, k:k + 1] for k in range(gates.shape[0])], axis=0)
        ow = D_MODEL // FFN_COL_BLOCKS
        for b in range(FFN_COL_BLOCKS):
            issue(FFN_COL_BLOCKS + b)
            o_ref[:, b * ow:(b + 1) * ow] = (jnp.dot(hmid, wd_ref[0, 0, :, b * ow:(b + 1) * ow],
                                                     preferred_element_type=f32) * gate).astype(bf16)

    @pl.when(s == n)
    def _():
        wait_rows(slot)
        wait_rows((s + 1) % FFN_SLOTS)
        o_ref[...] = jnp.zeros_like(o_ref)


def _ffn(tok, lists, rows, wg, wu, wd, layer, capacity, tc):
    nt = capacity // tc
    lt = tc // LANES
    n = N_EXPERTS * nt
    expert = lambda s: (layer, jnp.minimum(s // nt, N_EXPERTS - 1), 0, 0)
    return pl.pallas_call(
        _ffn_kernel,
        grid=(n + 1,),
        in_specs=[pl.BlockSpec((1, 1, tc), lambda s: (jnp.minimum(s, n - 1), 0, 0), memory_space=pltpu.SMEM),
                  pl.BlockSpec((1, 1, tc), lambda s: (jnp.minimum(s + 1, n - 1), 0, 0), memory_space=pltpu.SMEM),
                  pl.BlockSpec((1, 1, tc), lambda s: (jnp.minimum(s + 2, n - 1), 0, 0), memory_space=pltpu.SMEM),
                  pl.BlockSpec((1, lt, 8, LANES), lambda s: (jnp.minimum(s // nt, N_EXPERTS - 1), s % nt, 0, 0)),
                  pl.BlockSpec(memory_space=pl.ANY),
                  pl.BlockSpec((1, 1, D_MODEL, EXPERT_FF), expert), pl.BlockSpec((1, 1, D_MODEL, EXPERT_FF), expert),
                  pl.BlockSpec((1, 1, EXPERT_FF, D_MODEL), expert)],
        out_specs=pl.BlockSpec((tc, D_MODEL), lambda s: (s, 0)),
        out_shape=jax.ShapeDtypeStruct(((n + 1) * tc, D_MODEL), bf16),
        scratch_shapes=[pltpu.VMEM((FFN_SLOTS, tc, ROW_TILES, LANES), f32), pltpu.SemaphoreType.DMA((FFN_SLOTS,))],
        compiler_params=_cparams(("arbitrary",)),
        name="ffn",
    )(tok, tok, tok, lists, rows, wg, wu, wd)


WIN = 128
WIN_SMALL = 64
ROW_ALIGN = 16
SPARE = 2


def _combine_kernel(offs_ref, h_ref, rank_ref, ye_ref, y_ref, win, sem, *, capacity):
    c = pl.program_id(0)
    n_chunks = pl.num_programs(0)
    slot = c % 2

    def window_starts(chunk):
        return [offs_ref[chunk * N_EXPERTS + e] & -ROW_ALIGN for e in range(N_EXPERTS)]

    def reach(chunk, starts):
        far = jnp.int32(0)
        for e in range(N_EXPERTS):
            st = offs_ref[chunk * N_EXPERTS + e]
            cnt = offs_ref[(chunk + 1) * N_EXPERTS + e] - st
            far = jnp.maximum(far, jnp.where(cnt > 0, st - starts[e] + cnt, 0))
        return far

    def copy(e, start, p, k, w):
        src = jnp.minimum(e * capacity + start + p * w, ye_ref.shape[0] - w)
        return pltpu.make_async_copy(ye_ref.at[pl.ds(pl.multiple_of(src, ROW_ALIGN), w)],
                                     win.at[k, pl.ds(e * w, w)], sem.at[k])

    def fetch_first(chunk, k):
        st = window_starts(chunk)
        far = reach(chunk, st)
        for w, cond in ((WIN_SMALL, far <= WIN_SMALL), (WIN, far > WIN_SMALL)):
            @pl.when(cond)
            def _(w=w):
                for e in range(N_EXPERTS):
                    copy(e, st[e], 0, k, w).start()

    @pl.when(c == 0)
    def _():
        fetch_first(c, slot)

    @pl.when(c + 1 < n_chunks)
    def _():
        fetch_first(c + 1, 1 - slot)

    starts = window_starts(c)
    far = reach(c, starts)

    def scatter(w):
        width = N_EXPERTS * w
        col = lax.broadcasted_iota(jnp.int32, (N_EXPERTS, width), 1)
        row = lax.broadcasted_iota(jnp.int32, (N_EXPERTS, width), 0)
        log_w = w.bit_length() - 1
        spread = jnp.where(lax.shift_right_logical(col, log_w) == row, 1.0, 0.0).astype(bf16)
        rank_rep = jnp.dot(rank_ref[...].astype(bf16), spread, preferred_element_type=f32)
        col1 = lax.broadcasted_iota(jnp.int32, (1, width), 1)
        base = jnp.zeros((1, width), jnp.int32)
        for e in range(N_EXPERTS):
            base = jnp.where(lax.shift_right_logical(col1, log_w) == e, starts[e] - offs_ref[c * N_EXPERTS + e], base)
        base = (base + (col1 & (w - 1))).astype(f32)

        def scatter_matmul(p, k):
            hit = rank_rep == base + (p * w).astype(f32)
            sel = jnp.where(hit, 1.0, 0.0).astype(bf16)
            return jnp.dot(sel, win[k, 0:width, :], preferred_element_type=f32)

        def fetch_pass(p):
            for e in range(N_EXPERTS):
                copy(e, starts[e], p, SPARE, w).start()

        def wait_windows(k):
            for e in range(N_EXPERTS):
                copy(e, starts[e], 0, k, w).wait()

        if w == WIN_SMALL:
            wait_windows(slot)
            y_ref[...] = h_ref[...] + scatter_matmul(jnp.int32(0), slot)
            return
        n_pass = lax.shift_right_logical(far + (w - 1), w.bit_length() - 1)

        @pl.when(n_pass > 1)
        def _():
            fetch_pass(jnp.int32(1))

        wait_windows(slot)
        acc = scatter_matmul(jnp.int32(0), slot)

        def later_pass(p, acc):
            wait_windows(SPARE)
            res = scatter_matmul(p, SPARE)

            @pl.when(p + 1 < n_pass)
            def _():
                fetch_pass(p + 1)
            return acc + res

        y_ref[...] = h_ref[...] + lax.fori_loop(1, n_pass, later_pass, acc)

    @pl.when(far <= WIN_SMALL)
    def _():
        scatter(WIN_SMALL)

    @pl.when(far > WIN_SMALL)
    def _():
        scatter(WIN)


def _combine(offs, h, rank16, ye, capacity):
    t = h.shape[0]
    row = lambda c, offs: (c, 0)
    return pl.pallas_call(
        functools.partial(_combine_kernel, capacity=capacity),
        grid_spec=pltpu.PrefetchScalarGridSpec(
            num_scalar_prefetch=1,
            grid=(t // RANK_CHUNK,),
            in_specs=[pl.BlockSpec((RANK_CHUNK, D_MODEL), row), pl.BlockSpec((RANK_CHUNK, N_EXPERTS), row),
                      pl.BlockSpec(memory_space=pl.ANY)],
            out_specs=pl.BlockSpec((RANK_CHUNK, D_MODEL), row),
            scratch_shapes=[pltpu.VMEM((SPARE + 1, N_EXPERTS * WIN, D_MODEL), bf16),
                            pltpu.SemaphoreType.DMA((SPARE + 1,))]),
        out_shape=jax.ShapeDtypeStruct((t, D_MODEL), f32),
        compiler_params=_cparams(("arbitrary",)),
        name="combine",
    )(offs, h, rank16, ye)


def _rope_tables(seq):
    half = HEAD_DIM // 2
    inv = 1.0 / (ROPE_THETA ** (jnp.arange(0, HEAD_DIM, 2, dtype=f32) / HEAD_DIM))
    ang = jnp.arange(seq, dtype=f32)[:, None] * inv[None, :]
    cos = jnp.tile(jnp.cos(ang), (1, LANES // half))
    sign = jnp.where((jnp.arange(LANES) % HEAD_DIM) < half, -1.0, 1.0).astype(f32)
    sin = jnp.tile(jnp.sin(ang), (1, LANES // half)) * sign[None, :]
    return cos, sin


def _block_diag_mean(width):
    i = jnp.arange(width)
    return jnp.where((i[:, None] // HEAD_DIM) == (i[None, :] // HEAD_DIM), 1.0 / HEAD_DIM, 0.0).astype(bf16)


def _pick_tile(n, pref):
    while n % pref:
        pref //= 2
    return pref


def _trunk(x, p):
    batch, seq, _ = x.shape
    t = batch * seq
    capacity = CAPACITY_FACTOR * t // N_EXPERTS
    assert seq % (2 * BLOCK) == 0 and t % RANK_CHUNK == 0 and capacity % LANES == 0
    tb = _pick_tile(seq, 512)
    tq = _pick_tile(seq, 256)
    kc = _pick_tile(seq, DIFF_KEY_CHUNK)
    heads = DIFF_HEADS
    tc = _pick_tile(capacity, FFN_TILE)
    cos, sin = _rope_tables(seq)
    gm256, gm128 = _block_diag_mean(256), _block_diag_mean(128)
    x2 = x.reshape(t, D_MODEL)
    depth = p["w_in"].shape[0]
    for l in range(depth):
        lam_init = 0.8 - 0.6 * math.exp(-0.3 * l)
        lam = (jnp.exp(jnp.sum(p["lambda_q1"][l] * p["lambda_k1"][l]))
               - jnp.exp(jnp.sum(p["lambda_q2"][l] * p["lambda_k2"][l])) + lam_init)
        bound = (8.0 * LOG2E * (1.0 + 2.0 ** -6)) * jnp.max(jnp.abs(p["diff_q_norm"][l])) \
            * jnp.max(jnp.abs(p["diff_k_norm"][l]))
        scal = jnp.stack([lam, jnp.asarray(1.0 - lam_init, f32), bound]).astype(f32)
        tile8 = lambda v: jnp.tile(v, 8)[None, :]
        dq, dk, dv, sqp, sk, sv = _attn_in(
            x2, p["attn_norm"][l][None, :], p["w_in_bf"][l], cos, sin,
            tile8(p["diff_q_norm"][l]), tile8(p["diff_k_norm"][l]), tile8(p["swa_q_norm"][l]),
            jnp.tile(p["swa_k_norm"][l], 2)[None, :], gm256, gm128, seq, tb)
        do = _diff_attn(scal, dq, dk, dv, p["diff_subln"][l][None, :], batch, seq, tq, heads, kc)
        swa_bound = (8.0 * LOG2E * (1.0 + 2.0 ** -6)) * jnp.max(jnp.abs(p["swa_q_norm"][l])) \
            * jnp.max(jnp.abs(p["swa_k_norm"][l]))
        so = _band_attn(jnp.concatenate([p["swa_sink"][l] * LOG2E, swa_bound[None]]).astype(f32), sqp, sk, sv, batch, seq)
        h, rows, afft = _attn_out(do, so, x2, p["w_out_bf"][l], p["ffn_norm"][l][None, :],
                                   jnp.concatenate([p["r_hi"][l], p["r_lo"][l]], axis=1), p["r_hi"][l], tb)
        sel = _select(afft, capacity)
        tokbuf, rank16, offs, ids = _rank(afft, sel[:, 0:1], sel[:, 1:2], capacity)
        tok = ids[:, :capacity // LANES].reshape(N_EXPERTS * capacity // tc, 1, tc)
        ye = _ffn(tok, tokbuf, rows, p["w_gate_bf"], p["w_up_bf"], p["w_down_bf"], l, capacity, tc)
        x2 = _combine(offs, h, rank16, ye, capacity)
    return x2.reshape(batch, seq, D_MODEL)


def kernel(x_prompt, x_sample, attn_norm, w_in, diff_q_norm, diff_k_norm, lambda_q1, lambda_k1, lambda_q2, lambda_k2,
           diff_subln, swa_q_norm, swa_k_norm, swa_sink, w_out, ffn_norm, w_router, w_gate, w_up, w_down):
    r_pad = jnp.pad(w_router, ((0, 0), (0, 0), (0, LANES - N_EXPERTS)))
    r_hi = r_pad.astype(bf16)
    p = dict(attn_norm=attn_norm, diff_q_norm=diff_q_norm, diff_k_norm=diff_k_norm, lambda_q1=lambda_q1,
             lambda_k1=lambda_k1, lambda_q2=lambda_q2, lambda_k2=lambda_k2, diff_subln=diff_subln,
             swa_q_norm=swa_q_norm, swa_k_norm=swa_k_norm, swa_sink=swa_sink, ffn_norm=ffn_norm,
             w_in=w_in, w_in_bf=w_in.astype(bf16), w_out_bf=w_out.astype(bf16),
             r_hi=r_hi, r_lo=(r_pad - r_hi.astype(f32)).astype(bf16),
             w_gate_bf=w_gate.astype(bf16), w_up_bf=w_up.astype(bf16), w_down_bf=w_down.astype(bf16))
    return (_trunk(x_prompt, p), _trunk(x_sample, p))
```
